```python
import math
import jax
import jax.numpy as jnp
from jax import lax
import numpy as np

D_MODEL = 2048
BATCH = 2
SEQ = 8192
DEPTH = 4

HEAD_DIM = 64
EPS = 1e-6
NA_HEADS = 8
NA_KH = 8
NA_KW = 16
GRID_W = 64
WA_Q_HEADS = 8
WA_KV_HEADS = 2
WINDOW = 128
WA_BLOCK = 128
ROPE_THETA = 500000.0
ROPE_DIM = HEAD_DIM // 4
SSM_HEADS = 8
SSM_HEAD_DIM = 64
SSM_GROUPS = 2
SSM_STATE = 128
SSM_CONV = 5
SSM_CHUNK = 128
CONF_WIDTH = 512
CONF_KERNEL = 31
NA_WIDTH = NA_HEADS * HEAD_DIM
WA_Q_WIDTH = WA_Q_HEADS * HEAD_DIM
WA_KV_WIDTH = WA_KV_HEADS * HEAD_DIM
SSM_WIDTH = SSM_HEADS * SSM_HEAD_DIM
SSM_BC_WIDTH = SSM_GROUPS * SSM_STATE
SSM_XBC_WIDTH = SSM_WIDTH + 2 * SSM_BC_WIDTH
D_MIX = NA_WIDTH + WA_Q_WIDTH + SSM_WIDTH + CONF_WIDTH
IN_SPLITS = (NA_WIDTH, NA_WIDTH, NA_WIDTH, WA_Q_WIDTH, WA_KV_WIDTH, WA_KV_WIDTH,
             SSM_WIDTH, SSM_XBC_WIDTH, 2 * SSM_HEADS, 2 * CONF_WIDTH)
IN_WIDTH = (3 * NA_WIDTH + WA_Q_WIDTH + 2 * WA_KV_WIDTH + SSM_WIDTH + SSM_XBC_WIDTH
            + 2 * SSM_HEADS + 2 * CONF_WIDTH)
N_EXPERTS = 16
EC_CAPACITY = 2
EXPERT_FF = D_MODEL // 2

kernel_name = 'hybrid_na2d_swa_ssd_conv_ec_encoder'


def _rms(x, w):
    xf = x.astype(jnp.float32)
    y = xf * lax.rsqrt(jnp.mean(xf * xf, axis=-1, keepdims=True) + EPS)
    return (y * w.astype(jnp.float32)).astype(x.dtype)


def _layer_norm(x, w, b):
    xf = x.astype(jnp.float32)
    mu = jnp.mean(xf, axis=-1, keepdims=True)
    var = jnp.mean(jnp.square(xf - mu), axis=-1, keepdims=True)
    y = (xf - mu) * lax.rsqrt(var + EPS)
    return (y * w.astype(jnp.float32) + b.astype(jnp.float32)).astype(x.dtype)


def _dwconv(x, w, b):
    k = w.shape[0]
    left = (k - 1) // 2
    y = lax.conv_general_dilated(x, w[:, None, :].astype(x.dtype), window_strides=(1,),
                                 padding=[(left, k - 1 - left)],
                                 dimension_numbers=('NWC', 'WIO', 'NWC'),
                                 feature_group_count=x.shape[-1])
    return y + b.astype(x.dtype)


def _split_cols(u):
    offs = np.cumsum(np.array(IN_SPLITS))[:-1].tolist()
    return jnp.split(u, offs, axis=-1)


def _partial_rope(x, pos):
    half = ROPE_DIM // 2
    inv_freq = 1.0 / (ROPE_THETA ** (jnp.arange(half, dtype=jnp.float32) * 2.0 / ROPE_DIM))
    ang = pos.astype(jnp.float32)[:, None] * inv_freq[None, :]
    cos = jnp.cos(ang)[None, :, None, :]
    sin = jnp.sin(ang)[None, :, None, :]
    xf = x[..., :ROPE_DIM].astype(jnp.float32)
    x1, x2 = xf[..., :half], xf[..., half:]
    rot = jnp.concatenate([x1 * cos - x2 * sin, x2 * cos + x1 * sin], axis=-1).astype(x.dtype)
    return jnp.concatenate([rot, x[..., ROPE_DIM:]], axis=-1)


def _neighbourhood_attention(q, k, v, rpb):
    bsz, s, h, d = q.shape
    rows = s // GRID_W
    kh = min(NA_KH, rows)
    q = (q * d ** -0.5).reshape(bsz, rows, GRID_W, h, d)
    k = k.reshape(bsz, rows, GRID_W, h, d)
    v = v.reshape(bsz, rows, GRID_W, h, d)
    col = jnp.arange(GRID_W)
    col_idx = jnp.clip(col - NA_KW // 2, 0, GRID_W - NA_KW)[:, None] + jnp.arange(NA_KW)[None, :]
    bias_cols = rpb[:, :, col_idx - col[:, None] + NA_KW - 1]

    def one_row(r):
        r0 = jnp.clip(r - kh // 2, 0, rows - kh)
        k_win = lax.dynamic_slice_in_dim(k, r0, kh, axis=1)[:, :, col_idx]
        v_win = lax.dynamic_slice_in_dim(v, r0, kh, axis=1)[:, :, col_idx]
        q_r = lax.dynamic_index_in_dim(q, r, axis=1, keepdims=False)
        bias = bias_cols[:, r0 + jnp.arange(kh) - r + NA_KH - 1]
        logits = (jnp.einsum('bqhd,brqkhd->bhqrk', q_r, k_win).astype(jnp.float32)
                  + jnp.transpose(bias, (0, 2, 1, 3)).astype(jnp.float32))
        p = jax.nn.softmax(logits.reshape(bsz, h, GRID_W, kh * NA_KW), axis=-1).reshape(logits.shape)
        return jnp.einsum('bhqrk,brqkhd->bqhd', p.astype(v.dtype), v_win)

    out = lax.map(one_row, jnp.arange(rows))
    return jnp.transpose(out, (1, 0, 2, 3, 4)).reshape(bsz, s, h * d)


def _window_attention(q, k, v, sink):
    bsz, s, hq, d = q.shape
    hkv = k.shape[2]
    g = hq // hkv
    nb = s // WA_BLOCK
    qb = (q * d ** -0.5).reshape(bsz, nb, WA_BLOCK, hkv, g, d)

    def band(t):
        tp = jnp.pad(t, ((0, 0), (WA_BLOCK, WA_BLOCK), (0, 0), (0, 0))).reshape(bsz, nb + 2, WA_BLOCK, hkv, d)
        return jnp.concatenate([tp[:, :-2], tp[:, 1:-1], tp[:, 2:]], axis=2)

    kb, vb = band(k), band(v)
    qi = jnp.arange(WA_BLOCK)[:, None]
    kj = jnp.arange(3 * WA_BLOCK)[None, :]
    kpos = (jnp.arange(nb) * WA_BLOCK - WA_BLOCK)[:, None, None] + kj[None]
    mask = (jnp.abs(kj - WA_BLOCK - qi)[None] <= WINDOW) & (kpos >= 0) & (kpos < s)
    logits = jnp.einsum('bnqhgd,bnkhd->bhgnqk', qb, kb).astype(jnp.float32)
    logits = jnp.where(mask, logits, -jnp.inf)
    sink_l = jnp.broadcast_to(sink.astype(jnp.float32).reshape(1, hkv, g, 1, 1, 1), logits.shape[:-1] + (1,))
    p = jax.nn.softmax(jnp.concatenate([logits, sink_l], axis=-1), axis=-1)[..., :-1]
    o = jnp.einsum('bhgnqk,bnkhd->bnqhgd', p.astype(v.dtype), vb)
    return o.reshape(bsz, s, hq * d)


def _ssd(x, dt, a, bm, cm):
    bsz, s, h, p = x.shape
    L = SSM_CHUNK
    nc = s // L
    hg = h // SSM_GROUPS
    xc = (x * dt[..., None]).reshape(bsz, nc, L, SSM_GROUPS, hg, p)
    ac = (dt * a).reshape(bsz, nc, L, SSM_GROUPS, hg)
    bc = bm.reshape(bsz, nc, L, SSM_GROUPS, SSM_STATE)
    cc = cm.reshape(bsz, nc, L, SSM_GROUPS, SSM_STATE)
    a_cum = jnp.cumsum(ac, axis=2)
    tri = jnp.tril(jnp.ones((L, L), dtype=bool))[None, None, :, :, None, None]
    seg = a_cum[:, :, :, None] - a_cum[:, :, None, :]
    decay = jnp.exp(jnp.where(tri, seg, -jnp.inf))
    cb = jnp.einsum('bclgn,bcsgn->bclsg', cc, bc)
    y_diag = jnp.einsum('bclsg,bclsgh,bcsghp->bclghp', cb, decay, xc)
    decay_states = jnp.exp(a_cum[:, :, -1:] - a_cum)
    states = jnp.einsum('bclgn,bclgh,bclghp->bcghpn', bc, decay_states, xc)
    chunk_decay = jnp.exp(a_cum[:, :, -1])

    def step(hstate, inp):
        dec, st = inp
        return dec[..., None, None] * hstate + st, hstate

    h0 = jnp.zeros((bsz, SSM_GROUPS, hg, p, SSM_STATE), jnp.float32)
    _, prev = lax.scan(step, h0, (jnp.moveaxis(chunk_decay, 1, 0), jnp.moveaxis(states, 1, 0)))
    prev = jnp.moveaxis(prev, 0, 1)
    y_off = jnp.einsum('bclgn,bcghpn,bclgh->bclghp', cc, prev, jnp.exp(a_cum))
    return (y_diag + y_off).reshape(bsz, s, h, p)


def _expert_choice_ffn(h, w_router, w_gate, w_up, w_down):
    bsz, s, dm = h.shape
    cap = EC_CAPACITY * s // N_EXPERTS
    aff = jax.nn.softmax(jnp.einsum('bsd,de->bse', h, w_router).astype(jnp.float32), axis=-1)
    gate, idx = lax.top_k(jnp.swapaxes(aff, 1, 2), cap)
    xin = jax.vmap(lambda hb, ib: hb[ib])(h, idx)
    hid = jax.nn.silu(jnp.einsum('becd,edf->becf', xin, w_gate)) * jnp.einsum('becd,edf->becf', xin, w_up)
    out = jnp.einsum('becf,efd->becd', hid, w_down) * gate[..., None].astype(h.dtype)
    return jax.vmap(lambda ob, ib: jnp.zeros((s, dm), ob.dtype).at[ib.reshape(-1)].add(ob.reshape(-1, dm)))(out, idx)


def _mixing_sublayer(x, norm_w, w_in, na_q_norm, na_k_norm, na_rpb, wa_q_norm, wa_k_norm, wa_sink,
                     ssm_conv_w, ssm_conv_b, ssm_dt_bias, ssm_a_log, ssm_d, ssm_norm_w,
                     conf_dw_w, conf_dw_b, conf_ln_w, conf_ln_b, w_out):
    bsz, s, _ = x.shape
    hn = _rms(x, norm_w)
    u = jnp.einsum('bsd,de->bse', hn, w_in)
    na_q, na_k, na_v, wa_q, wa_k, wa_v, ssm_z, ssm_xbc, ssm_dt, conf_in = _split_cols(u)

    def heads(t):
        return t.reshape(bsz, s, -1, HEAD_DIM)

    o_a = _neighbourhood_attention(_rms(heads(na_q), na_q_norm), _rms(heads(na_k), na_k_norm), heads(na_v), na_rpb)

    pos = jnp.arange(s)
    o_b = _window_attention(_partial_rope(_rms(heads(wa_q), wa_q_norm), pos),
                            _partial_rope(_rms(heads(wa_k), wa_k_norm), pos),
                            heads(wa_v), wa_sink)

    xbc = jax.nn.silu(_dwconv(ssm_xbc, ssm_conv_w, ssm_conv_b)).astype(jnp.float32)
    xs = xbc[..., :SSM_WIDTH].reshape(bsz, s, SSM_HEADS, SSM_HEAD_DIM)
    bm = xbc[..., SSM_WIDTH:SSM_WIDTH + SSM_BC_WIDTH].reshape(bsz, s, SSM_GROUPS, SSM_STATE)
    cm = xbc[..., SSM_WIDTH + SSM_BC_WIDTH:].reshape(bsz, s, SSM_GROUPS, SSM_STATE)
    dt = jax.nn.softplus(ssm_dt.astype(jnp.float32).reshape(bsz, s, 2, SSM_HEADS) + ssm_dt_bias.astype(jnp.float32))
    a = -jnp.exp(ssm_a_log.astype(jnp.float32))

    def flip(t):
        return jnp.flip(t, axis=1)

    y_fwd = _ssd(xs, dt[:, :, 0], a[0], bm, cm)
    y_bwd = flip(_ssd(flip(xs), flip(dt[:, :, 1]), a[1], flip(bm), flip(cm)))
    y = (y_fwd + y_bwd + xs * ssm_d.astype(jnp.float32)[:, None]).reshape(bsz, s, SSM_WIDTH).astype(x.dtype)
    o_c = _rms(y * jax.nn.silu(ssm_z), ssm_norm_w)

    c_a, c_g = jnp.split(conf_in, 2, axis=-1)
    cv = _dwconv(c_a * jax.nn.sigmoid(c_g), conf_dw_w, conf_dw_b)
    o_d = jax.nn.silu(_layer_norm(cv, conf_ln_w, conf_ln_b))

    mixed = jnp.concatenate([o_a, o_b, o_c, o_d], axis=-1)
    return x + jnp.einsum('bse,ed->bsd', mixed, w_out)


def setup_inputs(seed: int = 0) -> dict:
    key = jax.random.key(seed)
    ks = jax.random.split(key, 26)
    f32 = jnp.float32
    L = DEPTH

    def nrm(k, shape, sc):
        return jax.random.normal(k, shape, f32) * sc

    def gain(k, shape):
        return 1.0 + 0.02 * jax.random.normal(k, shape, f32)

    dt0 = jnp.exp(jax.random.uniform(ks[12], (L, 2, SSM_HEADS), f32, math.log(1e-3), math.log(1e-1)))
    return {
        'x': jax.random.normal(ks[0], (BATCH, SEQ, D_MODEL), f32),
        'mix_norm_w': gain(ks[1], (L, D_MODEL)),
        'w_in': nrm(ks[2], (L, D_MODEL, IN_WIDTH), D_MODEL ** -0.5),
        'na_q_norm': gain(ks[3], (L, HEAD_DIM)),
        'na_k_norm': gain(ks[4], (L, HEAD_DIM)),
        'na_rpb': nrm(ks[5], (L, NA_HEADS, 2 * NA_KH - 1, 2 * NA_KW - 1), 0.1),
        'wa_q_norm': gain(ks[6], (L, HEAD_DIM)),
        'wa_k_norm': gain(ks[7], (L, HEAD_DIM)),
        'wa_sink': nrm(ks[8], (L, WA_Q_HEADS), 0.5),
        'ssm_conv_w': nrm(ks[9], (L, SSM_CONV, SSM_XBC_WIDTH), SSM_CONV ** -0.5),
        'ssm_conv_b': nrm(ks[10], (L, SSM_XBC_WIDTH), 0.02),
        'ssm_dt_bias': dt0 + jnp.log(-jnp.expm1(-dt0)),
        'ssm_a_log': jnp.log(jax.random.uniform(ks[13], (L, 2, SSM_HEADS), f32, 1.0, 16.0)),
        'ssm_d': gain(ks[14], (L, SSM_HEADS)),
        'ssm_norm_w': gain(ks[15], (L, SSM_WIDTH)),
        'conf_dw_w': nrm(ks[16], (L, CONF_KERNEL, CONF_WIDTH), CONF_KERNEL ** -0.5),
        'conf_dw_b': nrm(ks[17], (L, CONF_WIDTH), 0.02),
        'conf_ln_w': gain(ks[18], (L, CONF_WIDTH)),
        'conf_ln_b': nrm(ks[19], (L, CONF_WIDTH), 0.02),
        'w_out': nrm(ks[20], (L, D_MIX, D_MODEL), D_MIX ** -0.5),
        'ffn_norm_w': gain(ks[21], (L, D_MODEL)),
        'w_router': nrm(ks[22], (L, D_MODEL, N_EXPERTS), D_MODEL ** -0.5),
        'w_gate': nrm(ks[23], (L, N_EXPERTS, D_MODEL, EXPERT_FF), D_MODEL ** -0.5),
        'w_up': nrm(ks[24], (L, N_EXPERTS, D_MODEL, EXPERT_FF), D_MODEL ** -0.5),
        'w_down': nrm(ks[25], (L, N_EXPERTS, EXPERT_FF, D_MODEL), EXPERT_FF ** -0.5),
    }


def reference(x, mix_norm_w, w_in, na_q_norm, na_k_norm, na_rpb, wa_q_norm, wa_k_norm, wa_sink,
              ssm_conv_w, ssm_conv_b, ssm_dt_bias, ssm_a_log, ssm_d, ssm_norm_w,
              conf_dw_w, conf_dw_b, conf_ln_w, conf_ln_b, w_out,
              ffn_norm_w, w_router, w_gate, w_up, w_down):
    for l in range(DEPTH):
        x = _mixing_sublayer(x, mix_norm_w[l], w_in[l], na_q_norm[l], na_k_norm[l], na_rpb[l],
                             wa_q_norm[l], wa_k_norm[l], wa_sink[l],
                             ssm_conv_w[l], ssm_conv_b[l], ssm_dt_bias[l], ssm_a_log[l], ssm_d[l], ssm_norm_w[l],
                             conf_dw_w[l], conf_dw_b[l], conf_ln_w[l], conf_ln_b[l], w_out[l])
        x = x + _expert_choice_ffn(_rms(x, ffn_norm_w[l]), w_router[l], w_gate[l], w_up[l], w_down[l])
    return x
```

```python
import functools

import numpy as np
import jax
import jax.numpy as jnp
from jax import lax
from jax.experimental import pallas as pl
from jax.experimental.pallas import tpu as pltpu

F32 = jnp.float32
BF16 = jnp.bfloat16
I32 = jnp.int32
SDS = jax.ShapeDtypeStruct

D_MODEL = 2048
HEAD_DIM = 64
EPS = 1e-6
NA_HEADS = 8
NA_KH = 8
NA_KW = 16
GRID_W = 64
WA_Q_HEADS = 8
WA_KV_HEADS = 2
WINDOW = 128
ROPE_THETA = 500000.0
ROPE_DIM = HEAD_DIM // 4
SSM_HEADS = 8
SSM_STATE = 128
SSM_CONV = 5
SSM_CHUNK = 128
CONF_WIDTH = 512
CONF_KERNEL = 31
N_EXPERTS = 16
EC_CAPACITY = 2
EXPERT_FF = D_MODEL // 2

U_XBC = 0
U_CONF = 1024
U_NAQ = 2048
U_NAK = 2560
U_NAV = 3072
U_WAQ = 3584
U_Z = 4096
U_WAK = 4608
U_WAV = 4736
U_WIDTH = 4864
LANES = 128
VMEM_LIMIT = 56 * 1024 * 1024

NT_DIMS = (((1,), (1,)), ((), ()))
TN_DIMS = (((0,), (0,)), ((), ()))
NEG = -1e30


def _cparams(sem):
    return pltpu.CompilerParams(dimension_semantics=sem, vmem_limit_bytes=VMEM_LIMIT)


def _split3(v):
    a = v.astype(BF16)
    r = v - a.astype(F32)
    b = r.astype(BF16)
    c = (r - b.astype(F32)).astype(BF16)
    return a, b, c


def _sigmoid(v):
    return 1.0 / (1.0 + jnp.exp(-v))


def _head_rms(xf, w, bd):
    sq = xf * xf
    hi = sq.astype(BF16)
    lo = (sq - hi.astype(F32)).astype(BF16)
    ms = (jnp.dot(hi, bd, preferred_element_type=F32)
          + jnp.dot(lo, bd, preferred_element_type=F32))
    return xf * lax.rsqrt(ms + EPS) * w


def _inproj_body(x_ref, nw_ref, w_ref, wdt_ref, u_ref, dt_ref, h_scr):
    @pl.when(pl.program_id(1) == 0)
    def _():
        x = x_ref[...]
        ms = jnp.mean(x * x, axis=-1, keepdims=True)
        h = (x * lax.rsqrt(ms + EPS) * nw_ref[...]).astype(BF16)
        h_scr[...] = h
        dt_ref[...] = jnp.dot(h, wdt_ref[...], preferred_element_type=F32)

    u_ref[...] = jnp.dot(h_scr[...], w_ref[...], preferred_element_type=F32).astype(BF16)


def _in_proj(x2, norm_w, w_main, w_dt, tm=512):
    m = x2.shape[0]
    tn = U_WIDTH // 2
    return pl.pallas_call(
        _inproj_body,
        grid=(m // tm, U_WIDTH // tn),
        in_specs=[pl.BlockSpec((tm, D_MODEL), lambda i, j: (i, 0)),
                  pl.BlockSpec((1, D_MODEL), lambda i, j: (0, 0)),
                  pl.BlockSpec((D_MODEL, tn), lambda i, j: (0, j)),
                  pl.BlockSpec((D_MODEL, LANES), lambda i, j: (0, 0))],
        out_specs=[pl.BlockSpec((tm, tn), lambda i, j: (i, j)),
                   pl.BlockSpec((tm, LANES), lambda i, j: (i, 0))],
        out_shape=[SDS((m, U_WIDTH), BF16), SDS((m, LANES), F32)],
        scratch_shapes=[pltpu.VMEM((tm, D_MODEL), BF16)],
        compiler_params=_cparams(("parallel", "arbitrary")),
        name="in_proj",
    )(x2, norm_w, w_main, w_dt)


def _na_body(q_ref, k_ref, v_ref, qw_ref, kw_ref, bias_ref, bd_ref, o_ref, kn_scr, *, rq, wk, rows, r_blk, w_blk):
    s = q_ref.shape[0]
    nrb = rows // r_blk
    bd = bd_ref[...]
    ch = min(512, s)

    def kprep(i, c):
        s0 = pl.multiple_of(i * ch, ch)
        kf = k_ref[pl.ds(s0, ch), :].astype(F32)
        kn_scr[pl.ds(s0, ch), :] = _head_rms(kf, kw_ref[...], bd).astype(BF16)
        return c

    lax.fori_loop(0, s // ch, kprep, 0)
    lane = lax.broadcasted_iota(I32, (rq, LANES), 1)

    def blk(rb, c):
        q0 = pl.multiple_of(rb * rq, rq)
        w0 = pl.multiple_of(jnp.clip(rb * r_blk - NA_KH // 2, 0, rows - w_blk) * GRID_W, GRID_W)
        cls = jnp.where(rb == 0, 0, jnp.where(rb == nrb - 1, 2, 1))
        qn = _head_rms(q_ref[pl.ds(q0, rq), :].astype(F32), qw_ref[...], bd) * (HEAD_DIM ** -0.5)
        kw = kn_scr[pl.ds(w0, wk), :]
        vw = v_ref[pl.ds(w0, wk), :]
        outs = []
        for h in range(2):
            qh = jnp.where((lane // HEAD_DIM) == h, qn, 0.0).astype(BF16)
            lg = lax.dot_general(qh, kw, NT_DIMS, preferred_element_type=F32)
            lg = lg + bias_ref[h, cls].astype(F32)
            m = jnp.max(lg, axis=-1, keepdims=True)
            p = jnp.exp(lg - m)
            den = jnp.sum(p, axis=-1, keepdims=True)
            outs.append(jnp.dot(p.astype(BF16), vw, preferred_element_type=F32) * (1.0 / den))
        o_ref[pl.ds(q0, rq), :] = jnp.where(lane < HEAD_DIM, outs[0], outs[1]).astype(BF16)
        return c

    lax.fori_loop(0, nrb, blk, 0)


def _na_bias_table(rpb, r_blk, w_blk, rows):
    nrb = rows // r_blk
    col = np.arange(GRID_W)
    cstart = np.clip(col - NA_KW // 2, 0, GRID_W - NA_KW)
    valid_col = (col[None, :] >= cstart[:, None]) & (col[None, :] < cstart[:, None] + NA_KW)
    bj = np.clip(col[None, :] - col[:, None] + NA_KW - 1, 0, 2 * NA_KW - 2)
    tabs = []
    for rb in (0, min(1, nrb - 1), nrb - 1):
        r = rb * r_blk + np.arange(r_blk)
        w0 = np.clip(rb * r_blk - NA_KH // 2, 0, rows - w_blk)
        kr = w0 + np.arange(w_blk)
        r0 = np.clip(r - NA_KH // 2, 0, rows - NA_KH)
        valid_row = (kr[None, :] >= r0[:, None]) & (kr[None, :] < r0[:, None] + NA_KH)
        bi = np.clip(kr[None, :] - r[:, None] + NA_KH - 1, 0, 2 * NA_KH - 2)
        b = rpb[:, bi[:, None, :, None], bj[None, :, None, :]]
        mask = valid_row[:, None, :, None] & valid_col[None, :, None, :]
        b = jnp.where(mask[None], b, NEG)
        tabs.append(b.reshape(NA_HEADS, r_blk * GRID_W, w_blk * GRID_W))
    t = jnp.stack(tabs, axis=1)
    return t.reshape(NA_HEADS // 2, 2, 3, r_blk * GRID_W, w_blk * GRID_W).astype(BF16)


def _na_attention(u3, qw, kw, bias_tab, bd, r_blk, w_blk):
    b, s, _ = u3.shape
    rows = s // GRID_W
    rq, wk = r_blk * GRID_W, w_blk * GRID_W
    body = functools.partial(_na_body, rq=rq, wk=wk, rows=rows, r_blk=r_blk, w_blk=w_blk)
    return pl.pallas_call(
        body,
        grid=(b, NA_HEADS // 2),
        in_specs=[pl.BlockSpec((None, s, LANES), lambda i, h: (i, 0, U_NAQ // LANES + h)),
                  pl.BlockSpec((None, s, LANES), lambda i, h: (i, 0, U_NAK // LANES + h)),
                  pl.BlockSpec((None, s, LANES), lambda i, h: (i, 0, U_NAV // LANES + h)),
                  pl.BlockSpec((1, LANES), lambda i, h: (0, 0)),
                  pl.BlockSpec((1, LANES), lambda i, h: (0, 0)),
                  pl.BlockSpec((None, 2, 3, rq, wk), lambda i, h: (h, 0, 0, 0, 0)),
                  pl.BlockSpec((LANES, LANES), lambda i, h: (0, 0))],
        out_specs=pl.BlockSpec((None, s, LANES), lambda i, h: (i, 0, h)),
        out_shape=SDS((b, s, NA_HEADS * HEAD_DIM), BF16),
        scratch_shapes=[pltpu.VMEM((s, LANES), BF16)],
        compiler_params=_cparams(("parallel", "parallel")),
        name="na2d",
    )(u3, u3, u3, qw, kw, bias_tab, bd)


def _rope(xf, cos, sin):
    lane = lax.broadcasted_iota(I32, xf.shape, 1)
    half = ROPE_DIM // 2
    partner = jnp.where((lane % HEAD_DIM) < half, pltpu.roll(xf, LANES - half, 1), pltpu.roll(xf, half, 1))
    return xf * cos + partner * sin


def _wa_body(sink_ref, q_ref, k_ref, v_ref, cos_ref, sin_ref, qw_ref, kw_ref, bd_ref, o_ref, kd_scr, vd_scr,
             *, qb, wkb):
    s = k_ref.shape[0]
    kvh = pl.program_id(1)
    n = pl.program_id(2)
    bd = bd_ref[...]
    ch = min(512, s)

    @pl.when(n == 0)
    def _():
        lane = lax.broadcasted_iota(I32, (ch, LANES), 1)
        mine = (lane // HEAD_DIM) == kvh

        def prep(i, c):
            s0 = pl.multiple_of(i * ch, ch)
            kf = k_ref[pl.ds(s0, ch), :].astype(F32)
            kf = _rope(_head_rms(kf, kw_ref[...], bd), cos_ref[pl.ds(s0, ch), :], sin_ref[pl.ds(s0, ch), :])
            kd_scr[pl.ds(s0, ch), :] = jnp.where(mine, kf, pltpu.roll(kf, HEAD_DIM, 1)).astype(BF16)
            vf = v_ref[pl.ds(s0, ch), :].astype(F32)
            vd_scr[pl.ds(s0, ch), :] = jnp.where(mine, vf, pltpu.roll(vf, HEAD_DIM, 1)).astype(BF16)
            return c

        lax.fori_loop(0, s // ch, prep, 0)

    q0 = pl.multiple_of(n * qb, qb)
    ks = pl.multiple_of(jnp.clip(n * qb - WINDOW, 0, s - wkb), LANES)
    cosq = cos_ref[pl.ds(q0, qb), :]
    sinq = sin_ref[pl.ds(q0, qb), :]
    kw = kd_scr[pl.ds(ks, wkb), :]
    vw = vd_scr[pl.ds(ks, wkb), :]
    qpos = q0 + lax.broadcasted_iota(I32, (qb, wkb), 0)
    kpos = ks + lax.broadcasted_iota(I32, (qb, wkb), 1)
    allowed = jnp.abs(kpos - qpos) <= WINDOW
    lane = lax.broadcasted_iota(I32, (qb, LANES), 1)
    for half in range(2):
        qf = q_ref[:, half * LANES:(half + 1) * LANES].astype(F32)
        qn = _rope(_head_rms(qf, qw_ref[...], bd), cosq, sinq) * (HEAD_DIM ** -0.5)
        outs = []
        for gg in range(2):
            qh = jnp.where((lane // HEAD_DIM) == gg, qn, 0.0).astype(BF16)
            lg = lax.dot_general(qh, kw, NT_DIMS, preferred_element_type=F32)
            lg = jnp.where(allowed, lg, NEG)
            sk = sink_ref[0, kvh * 4 + half * 2 + gg]
            m = jnp.maximum(jnp.max(lg, axis=-1, keepdims=True), sk)
            p = jnp.exp(lg - m)
            den = jnp.sum(p, axis=-1, keepdims=True) + jnp.exp(sk - m)
            outs.append(jnp.dot(p.astype(BF16), vw, preferred_element_type=F32) * (1.0 / den))
        o_ref[:, half * LANES:(half + 1) * LANES] = jnp.where(lane < HEAD_DIM, outs[0], outs[1]).astype(BF16)


def _wa_attention(u3, sink, cos_t, sin_t, qw, kw, bd, qb):
    b, s, _ = u3.shape
    wkb = qb + 2 * WINDOW
    body = functools.partial(_wa_body, qb=qb, wkb=wkb)
    return pl.pallas_call(
        body,
        grid=(b, WA_KV_HEADS, s // qb),
        in_specs=[pl.BlockSpec(memory_space=pltpu.SMEM),
                  pl.BlockSpec((None, qb, 2 * LANES), lambda i, h, n: (i, n, U_WAQ // (2 * LANES) + h)),
                  pl.BlockSpec((None, s, LANES), lambda i, h, n: (i, 0, U_WAK // LANES)),
                  pl.BlockSpec((None, s, LANES), lambda i, h, n: (i, 0, U_WAV // LANES)),
                  pl.BlockSpec((s, LANES), lambda i, h, n: (0, 0)),
                  pl.BlockSpec((s, LANES), lambda i, h, n: (0, 0)),
                  pl.BlockSpec((1, LANES), lambda i, h, n: (0, 0)),
                  pl.BlockSpec((1, LANES), lambda i, h, n: (0, 0)),
                  pl.BlockSpec((LANES, LANES), lambda i, h, n: (0, 0))],
        out_specs=pl.BlockSpec((None, qb, 2 * LANES), lambda i, h, n: (i, n, h)),
        out_shape=SDS((b, s, WA_Q_HEADS * HEAD_DIM), BF16),
        scratch_shapes=[pltpu.VMEM((s, LANES), BF16), pltpu.VMEM((s, LANES), BF16)],
        compiler_params=_cparams(("parallel", "arbitrary", "arbitrary")),
        name="swa_gqa",
    )(sink, u3, u3, u3, cos_t, sin_t, qw, kw, bd)


def _rope_tables(s):
    half = ROPE_DIM // 2
    inv_freq = 1.0 / (ROPE_THETA ** (jnp.arange(half, dtype=F32) * 2.0 / ROPE_DIM))
    ang = jnp.arange(s, dtype=F32)[:, None] * inv_freq[None, :]
    cos, sin = jnp.cos(ang), jnp.sin(ang)
    ones = jnp.ones((s, HEAD_DIM - ROPE_DIM), F32)
    cos_h = jnp.concatenate([cos, cos, ones], axis=1)
    sin_h = jnp.concatenate([-sin, sin, 0.0 * ones], axis=1)
    return jnp.tile(cos_h, (1, LANES // HEAD_DIM)), jnp.tile(sin_h, (1, LANES // HEAD_DIM))


def _ssd_body(*refs, reverse, nc, final):
    if final:
        (xp_ref, xc_ref, xn_ref, dt_ref, cw_ref, cb_ref, dtb_ref, a_ref, eh_ref,
         yf_ref, z_ref, dv_ref, nw_ref, o_ref, st_scr, cv_scr) = refs
    else:
        (xp_ref, xc_ref, xn_ref, dt_ref, cw_ref, cb_ref, dtb_ref, a_ref, eh_ref,
         o_ref, st_scr, cv_scr) = refs
    ci = pl.program_id(1)
    c = (nc - 1 - ci) if reverse else ci
    L = SSM_CHUNK
    sw = SSM_HEADS * HEAD_DIM

    @pl.when(ci == 0)
    def _():
        st_scr[...] = jnp.zeros_like(st_scr)

    has_prev = jnp.where(c > 0, 1.0, 0.0)
    has_next = jnp.where(c < nc - 1, 1.0, 0.0)
    cv_scr[0:8, :] = xp_ref[L - 8:L, :].astype(F32) * has_prev
    cv_scr[8:8 + L, :] = xc_ref[...].astype(F32)
    cv_scr[8 + L:16 + L, :] = xn_ref[0:8, :].astype(F32) * has_next
    left = (SSM_CONV - 1) // 2
    acc = cb_ref[...] + cw_ref[0:1, :] * cv_scr[pl.ds(8 - left, L), :]
    for k in range(1, SSM_CONV):
        acc = acc + cw_ref[k:k + 1, :] * cv_scr[pl.ds(8 - left + k, L), :]
    xa = acc * _sigmoid(acc)
    xs = xa[:, 0:sw]

    raw = dt_ref[...] + dtb_ref[...]
    dtv = jnp.maximum(raw, 0.0) + jnp.log1p(jnp.exp(-jnp.abs(raw)))
    da = dtv * a_ref[...]
    ri = lax.broadcasted_iota(I32, (L, L), 0)
    cj = lax.broadcasted_iota(I32, (L, L), 1)
    keep = (cj >= ri) if reverse else (cj <= ri)
    tri = jnp.where(keep, 1.0, 0.0).astype(BF16)
    d1, d2, d3 = _split3(da)
    cum = (jnp.dot(tri, d1, preferred_element_type=F32) + jnp.dot(tri, d2, preferred_element_type=F32)
           + jnp.dot(tri, d3, preferred_element_type=F32))
    cum_t = cum.T
    eh = eh_ref[...]
    c1, c2, c3 = _split3(cum)
    cum_x = (jnp.dot(c1, eh, preferred_element_type=F32) + jnp.dot(c2, eh, preferred_element_type=F32)
             + jnp.dot(c3, eh, preferred_element_type=F32))
    t1, t2, t3 = _split3(dtv)
    dt_x = (jnp.dot(t1, eh, preferred_element_type=F32) + jnp.dot(t2, eh, preferred_element_type=F32)
            + jnp.dot(t3, eh, preferred_element_type=F32))
    tot_x = cum_x[0:1, :] if reverse else cum_x[L - 1:L, :]
    xc = xs * dt_x
    xdec = (xc * jnp.exp(tot_x - cum_x)).astype(BF16)
    xcb = xc.astype(BF16)
    ecum = jnp.exp(cum_x)
    etot = jnp.exp(tot_x)
    lane = lax.broadcasted_iota(I32, (L, LANES), 1)
    doff = SSM_HEADS if reverse else 0
    for g in range(2):
        bg = xa[:, sw + g * SSM_STATE: sw + (g + 1) * SSM_STATE]
        cg = xa[:, sw + 2 * SSM_STATE + g * SSM_STATE: sw + 2 * SSM_STATE + (g + 1) * SSM_STATE]
        cgb = cg.astype(BF16)
        cb = lax.dot_general(cgb, bg.astype(BF16), NT_DIMS, preferred_element_type=F32)
        bgt = bg.T.astype(BF16)
        for pp in range(2):
            p = g * 2 + pp
            sl = slice(p * LANES, (p + 1) * LANES)
            ys = []
            for hh in range(2):
                j = doff + 2 * p + hh
                seg = cum[:, j:j + 1] - cum_t[j:j + 1, :]
                dec = jnp.exp(jnp.where(keep, seg, NEG))
                ys.append(jnp.dot((cb * dec).astype(BF16), xcb[:, sl], preferred_element_type=F32))
            y = jnp.where(lane < HEAD_DIM, ys[0], ys[1])
            prev = st_scr[p]
            y = y + jnp.dot(cgb, prev.astype(BF16), preferred_element_type=F32) * ecum[:, sl]
            st_scr[p] = etot[:, sl] * prev + jnp.dot(bgt, xdec[:, sl], preferred_element_type=F32)
            if final:
                y = y + yf_ref[:, sl] + xs[:, sl] * dv_ref[:, sl]
                zf = z_ref[:, sl].astype(F32)
                cv_scr[0:L, sl] = y * (zf * _sigmoid(zf))
            else:
                o_ref[:, sl] = y
    if final:
        gated = cv_scr[0:L, 0:sw]
        ms = jnp.mean(gated * gated, axis=-1, keepdims=True)
        o_ref[...] = (gated * lax.rsqrt(ms + EPS) * nw_ref[...]).astype(BF16)


def _ssd_pass(u3, dt3, cw, cb, dtb, a_vec, eh, extra, reverse):
    b, s, _ = u3.shape
    L = SSM_CHUNK
    nc = s // L
    sw = SSM_HEADS * HEAD_DIM
    final = extra is not None

    def cidx(ci):
        return (nc - 1 - ci) if reverse else ci

    xw = 2 * sw
    in_specs = [pl.BlockSpec((None, L, xw), lambda i, ci: (i, jnp.maximum(cidx(ci) - 1, 0), U_XBC // xw)),
                pl.BlockSpec((None, L, xw), lambda i, ci: (i, cidx(ci), U_XBC // xw)),
                pl.BlockSpec((None, L, xw), lambda i, ci: (i, jnp.minimum(cidx(ci) + 1, nc - 1), U_XBC // xw)),
                pl.BlockSpec((None, L, LANES), lambda i, ci: (i, cidx(ci), 0)),
                pl.BlockSpec((8, xw), lambda i, ci: (0, 0)),
                pl.BlockSpec((1, xw), lambda i, ci: (0, 0)),
                pl.BlockSpec((1, LANES), lambda i, ci: (0, 0)),
                pl.BlockSpec((1, LANES), lambda i, ci: (0, 0)),
                pl.BlockSpec((LANES, sw), lambda i, ci: (0, 0))]
    args = [u3, u3, u3, dt3, cw, cb, dtb, a_vec, eh]
    if final:
        y_f, dvec, nw = extra
        in_specs += [pl.BlockSpec((None, L, sw), lambda i, ci: (i, cidx(ci), 0)),
                     pl.BlockSpec((None, L, sw), lambda i, ci: (i, cidx(ci), U_Z // sw)),
                     pl.BlockSpec((1, sw), lambda i, ci: (0, 0)),
                     pl.BlockSpec((1, sw), lambda i, ci: (0, 0))]
        args += [y_f, u3, dvec, nw]
    body = functools.partial(_ssd_body, reverse=reverse, nc=nc, final=final)
    return pl.pallas_call(
        body,
        grid=(b, nc),
        in_specs=in_specs,
        out_specs=pl.BlockSpec((None, L, sw), lambda i, ci: (i, cidx(ci), 0)),
        out_shape=SDS((b, s, sw), BF16 if final else F32),
        scratch_shapes=[pltpu.VMEM((4, SSM_STATE, LANES), F32), pltpu.VMEM((L + 16, xw), F32)],
        compiler_params=_cparams(("parallel", "arbitrary")),
        name="ssd_bwd" if reverse else "ssd_fwd",
    )(*args)


def _conf_body(xp_ref, xc_ref, xn_ref, w_ref, b_ref, lnw_ref, lnb_ref, o_ref, scr, *, tc, nt):
    i = pl.program_id(1)
    cwid = CONF_WIDTH
    halo = 16

    def glu(v):
        return v[:, 0:cwid].astype(F32) * _sigmoid(v[:, cwid:2 * cwid].astype(F32))

    scr[0:halo, :] = glu(xp_ref[tc - halo:tc, :]) * jnp.where(i > 0, 1.0, 0.0)
    scr[halo:halo + tc, :] = glu(xc_ref[...])
    scr[halo + tc:2 * halo + tc, :] = glu(xn_ref[0:halo, :]) * jnp.where(i < nt - 1, 1.0, 0.0)
    left = (CONF_KERNEL - 1) // 2
    acc = b_ref[...] + w_ref[0:1, :] * scr[pl.ds(halo - left, tc), :]
    for k in range(1, CONF_KERNEL):
        acc = acc + w_ref[k:k + 1, :] * scr[pl.ds(halo - left + k, tc), :]
    mu = jnp.mean(acc, axis=-1, keepdims=True)
    cen = acc - mu
    var = jnp.mean(cen * cen, axis=-1, keepdims=True)
    y = cen * lax.rsqrt(var + EPS) * lnw_ref[...] + lnb_ref[...]
    o_ref[...] = (y * _sigmoid(y)).astype(BF16)


def _conformer(u3, w, bvec, lnw, lnb, tc):
    b, s, _ = u3.shape
    nt = s // tc
    cw2 = 2 * CONF_WIDTH
    body = functools.partial(_conf_body, tc=tc, nt=nt)
    return pl.pallas_call(
        body,
        grid=(b, nt),
        in_specs=[pl.BlockSpec((None, tc, cw2), lambda i, t: (i, jnp.maximum(t - 1, 0), U_CONF // cw2)),
                  pl.BlockSpec((None, tc, cw2), lambda i, t: (i, t, U_CONF // cw2)),
                  pl.BlockSpec((None, tc, cw2), lambda i, t: (i, jnp.minimum(t + 1, nt - 1), U_CONF // cw2)),
                  pl.BlockSpec((32, CONF_WIDTH), lambda i, t: (0, 0)),
                  pl.BlockSpec((1, CONF_WIDTH), lambda i, t: (0, 0)),
                  pl.BlockSpec((1, CONF_WIDTH), lambda i, t: (0, 0)),
                  pl.BlockSpec((1, CONF_WIDTH), lambda i, t: (0, 0))],
        out_specs=pl.BlockSpec((None, tc, CONF_WIDTH), lambda i, t: (i, t, 0)),
        out_shape=SDS((b, s, CONF_WIDTH), BF16),
        scratch_shapes=[pltpu.VMEM((tc + 32, CONF_WIDTH), F32)],
        compiler_params=_cparams(("parallel", "parallel")),
        name="conformer",
    )(u3, u3, u3, w, bvec, lnw, lnb)


def _outproj_body(oa_ref, ob_ref, oc_ref, od_ref, w_ref, x_ref, fw_ref, wrh_ref, wrl_ref, xo_ref, h_ref, lg_ref):
    acc = x_ref[...]
    for g, r in enumerate((oa_ref, ob_ref, oc_ref, od_ref)):
        acc = acc + jnp.dot(r[...], w_ref[g], preferred_element_type=F32)
    xo_ref[...] = acc
    ms = jnp.mean(acc * acc, axis=-1, keepdims=True)
    hf = acc * lax.rsqrt(ms + EPS) * fw_ref[...]
    hb = hf.astype(BF16)
    h_ref[...] = hb
    lo = (hf - hb.astype(F32)).astype(BF16)
    wrh = wrh_ref[...]
    lg_ref[...] = (lax.dot_general(wrh, hb, NT_DIMS, preferred_element_type=F32)
                   + lax.dot_general(wrh, lo, NT_DIMS, preferred_element_type=F32)
                   + lax.dot_general(wrl_ref[...], hb, NT_DIMS, preferred_element_type=F32))


def _out_proj(o_a, o_b, o_c, o_d, w4, x3, fw, wrh, wrl, tm):
    b, s, _ = x3.shape
    gw = 512
    mix_spec = pl.BlockSpec((None, tm, gw), lambda i, t: (i, t, 0))
    return pl.pallas_call(
        _outproj_body,
        grid=(b, s // tm),
        in_specs=[mix_spec, mix_spec, mix_spec, mix_spec,
                  pl.BlockSpec((4, gw, D_MODEL), lambda i, t: (0, 0, 0)),
                  pl.BlockSpec((None, tm, D_MODEL), lambda i, t: (i, t, 0)),
                  pl.BlockSpec((1, D_MODEL), lambda i, t: (0, 0)),
                  pl.BlockSpec((N_EXPERTS, D_MODEL), lambda i, t: (0, 0)),
                  pl.BlockSpec((N_EXPERTS, D_MODEL), lambda i, t: (0, 0))],
        out_specs=[pl.BlockSpec((None, tm, D_MODEL), lambda i, t: (i, t, 0)),
                   pl.BlockSpec((None, tm, D_MODEL), lambda i, t: (i, t, 0)),
                   pl.BlockSpec((None, N_EXPERTS, tm), lambda i, t: (i, 0, t))],
        out_shape=[SDS((b, s, D_MODEL), F32), SDS((b, s, D_MODEL), BF16), SDS((b, N_EXPERTS, s), F32)],
        compiler_params=_cparams(("parallel", "parallel")),
        name="out_proj",
    )(o_a, o_b, o_c, o_d, w4, x3, fw, wrh, wrl)


def _route_body(lg_ref, posm_ref, gate_ref, rng_ref, m_scr, c_scr, *, cap, ts, ntile):
    e, s = lg_ref.shape
    cb = 256
    nblk = s // cb
    lg = lg_ref[...]
    mx = jnp.max(lg, axis=0, keepdims=True)
    ex = jnp.exp(lg - mx)
    aff = ex / jnp.sum(ex, axis=0, keepdims=True)
    gate_ref[...] = aff
    bits = pltpu.bitcast(aff, I32)

    def search(i, v):
        cand = v | jnp.left_shift(jnp.int32(1), 30 - i)
        cnt = jnp.sum(jnp.where(bits >= cand, 1.0, 0.0), axis=1, keepdims=True)
        return jnp.where(cnt >= cap, cand, v)

    thr = lax.fori_loop(0, 31, search, jnp.zeros((e, 1), I32))
    gt = bits > thr
    eq = bits == thr
    need = cap - jnp.sum(jnp.where(gt, 1.0, 0.0), axis=1, keepdims=True)
    ri = lax.broadcasted_iota(I32, (cb, cb), 0)
    cj = lax.broadcasted_iota(I32, (cb, cb), 1)
    tri = jnp.where(ri <= cj, 1.0, 0.0).astype(BF16)

    def cumsum_into_c():
        def blk(i, carry):
            o = pl.multiple_of(i * cb, cb)
            inc = jnp.dot(m_scr[:, pl.ds(o, cb)].astype(BF16), tri, preferred_element_type=F32) + carry
            c_scr[:, pl.ds(o, cb)] = inc
            return inc[:, cb - 1:cb]
        lax.fori_loop(0, nblk, blk, jnp.zeros((e, 1), F32))

    eqf = jnp.where(eq, 1.0, 0.0)
    m_scr[...] = eqf
    cumsum_into_c()
    sel = gt | (eq & ((c_scr[...] - eqf) < need))
    self = jnp.where(sel, 1.0, 0.0)
    m_scr[...] = self
    cumsum_into_c()
    incl = c_scr[...]
    posm_ref[...] = jnp.where(sel, (incl - self).astype(I32), -1)
    lane = lax.broadcasted_iota(I32, (e, LANES), 1)
    rng = jnp.zeros((e, LANES), I32)
    for j in range(ntile):
        lo = jnp.sum(jnp.where(incl <= j * ts, 1.0, 0.0), axis=1, keepdims=True).astype(I32)
        hi = jnp.sum(jnp.where(incl < (j + 1) * ts, 1.0, 0.0), axis=1, keepdims=True).astype(I32)
        rng = jnp.where(lane == j, lo, rng)
        rng = jnp.where(lane == ntile + j, hi, rng)
    rng_ref[...] = rng


def _route(logits_t, cap, ts):
    b, e, s = logits_t.shape
    ntile = cap // ts
    body = functools.partial(_route_body, cap=cap, ts=ts, ntile=ntile)
    return pl.pallas_call(
        body,
        grid=(b,),
        in_specs=[pl.BlockSpec((None, e, s), lambda i: (i, 0, 0))],
        out_specs=[pl.BlockSpec((None, e, s), lambda i: (i, 0, 0)),
                   pl.BlockSpec((None, e, s), lambda i: (i, 0, 0)),
                   pl.BlockSpec((None, e, LANES), lambda i: (i, 0, 0))],
        out_shape=[SDS((b, e, s), I32), SDS((b, e, s), F32), SDS((b, e, LANES), I32)],
        scratch_shapes=[pltpu.VMEM((e, s), F32), pltpu.VMEM((e, s), F32)],
        compiler_params=_cparams(("parallel",)),
        name="route",
    )(logits_t)


def _visit_tables(rng, s, cap, ts, tk):
    b, e, _ = rng.shape
    ntile = cap // ts
    nchunk = s // tk
    clo = rng[:, :, 0:ntile] // tk
    chi = rng[:, :, ntile:2 * ntile] // tk
    cnt = chi - clo + 1
    cum = jnp.cumsum(cnt, axis=-1)
    start = cum - cnt
    vg = nchunk + ntile - 1
    v = jnp.arange(vg, dtype=I32)
    tile = jnp.minimum(jnp.sum((v[None, None, :, None] >= cum[:, :, None, :]).astype(I32), axis=-1), ntile - 1)
    tsel = jax.nn.one_hot(tile, ntile, dtype=I32)
    pick = lambda a: jnp.sum(tsel * a[:, :, None, :], axis=-1)
    chunk = jnp.minimum(pick(clo) + (v[None, None, :] - pick(start)), pick(chi))
    valid = (v[None, None, :] < cum[:, :, -1:]).astype(I32)
    first = (v[None, None, :] == pick(start)).astype(I32) * valid
    last = (v[None, None, :] == pick(cum) - 1).astype(I32) * valid
    g_meta = jnp.stack([tile, chunk, first, last, valid], axis=-1).reshape(-1).astype(I32)
    cidx = jnp.arange(nchunk, dtype=I32)
    hit = (clo[:, None] <= cidx[None, :, None, None]) & (cidx[None, :, None, None] <= chi[:, None])
    dummy = (jnp.arange(e)[:, None] == 0) & (jnp.arange(ntile)[None, :] == 0)
    hit = (hit | dummy[None, None]).reshape(b, -1)
    nflat = hit.shape[1]
    rank = jnp.cumsum(hit.astype(I32), axis=-1) - 1
    vc = e * (nchunk + ntile - 1) + nchunk
    total = rank[:, -1:] + 1
    vq = jnp.minimum(jnp.arange(vc, dtype=I32)[None, :], total - 1)
    onehot = (rank[:, None, :] == vq[:, :, None]) & hit[:, None, :]
    flat = jnp.sum(onehot.astype(I32) * jnp.arange(nflat, dtype=I32)[None, None, :], axis=-1)
    c_chunk = flat // (e * ntile)
    c_exp = (flat // ntile) % e
    c_tile = flat % ntile
    c_valid = (jnp.arange(vc, dtype=I32)[None, :] < total).astype(I32)
    prev_chunk = jnp.concatenate([jnp.full((b, 1), -1, I32), c_chunk[:, :-1]], axis=1)
    c_first = (c_chunk != prev_chunk).astype(I32)
    c_meta = jnp.stack([c_chunk, c_exp, c_tile, c_first, c_valid], axis=-1).reshape(-1).astype(I32)
    return g_meta, vg, c_meta, vc


def _gather_body(meta_ref, posm_ref, h_ref, o_ref, acc, *, vg, ts):
    bi, ei, vi = pl.program_id(0), pl.program_id(1), pl.program_id(2)
    base = ((bi * pl.num_programs(1) + ei) * vg + vi) * 5
    tile = meta_ref[base]

    @pl.when(meta_ref[base + 2] == 1)
    def _():
        acc[...] = jnp.zeros_like(acc)

    @pl.when(meta_ref[base + 4] == 1)
    def _():
        pos = posm_ref[...]
        slot = tile * ts + lax.broadcasted_iota(I32, (ts, pos.shape[1]), 0)
        onehot = jnp.where(pos == slot, 1.0, 0.0).astype(BF16)
        acc[...] += jnp.dot(onehot, h_ref[...], preferred_element_type=F32)

    @pl.when(meta_ref[base + 3] == 1)
    def _():
        o_ref[...] = acc[...].astype(BF16)


def _gather(g_meta, vg, posm4, h3, cap, ts, tk):
    b, e = posm4.shape[0], posm4.shape[1]

    def meta_at(bi, ei, vi, k):
        return ((bi * e + ei) * vg + vi) * 5 + k

    body = functools.partial(_gather_body, vg=vg, ts=ts)
    grid_spec = pltpu.PrefetchScalarGridSpec(
        num_scalar_prefetch=1,
        grid=(b, e, vg),
        in_specs=[pl.BlockSpec((None, None, 1, tk), lambda bi, ei, vi, m: (bi, ei, 0, m[meta_at(bi, ei, vi, 1)])),
                  pl.BlockSpec((None, tk, D_MODEL), lambda bi, ei, vi, m: (bi, m[meta_at(bi, ei, vi, 1)], 0))],
        out_specs=pl.BlockSpec((None, None, ts, D_MODEL), lambda bi, ei, vi, m: (bi, ei, m[meta_at(bi, ei, vi, 0)], 0)),
        scratch_shapes=[pltpu.VMEM((ts, D_MODEL), F32)],
    )
    return pl.pallas_call(
        body,
        grid_spec=grid_spec,
        out_shape=SDS((b, e, cap, D_MODEL), BF16),
        compiler_params=_cparams(("parallel", "parallel", "arbitrary")),
        name="moe_gather",
    )(g_meta, posm4, h3)


def _ffn_body(x_ref, wg_ref, wu_ref, wd_ref, o_ref, acc):
    f = pl.program_id(2)
    x = x_ref[...]
    g = jnp.dot(x, wg_ref[...].astype(BF16), preferred_element_type=F32)
    u = jnp.dot(x, wu_ref[...].astype(BF16), preferred_element_type=F32)
    hid = (g * _sigmoid(g) * u).astype(BF16)
    y = jnp.dot(hid, wd_ref[...].astype(BF16), preferred_element_type=F32)

    @pl.when(f == 0)
    def _():
        acc[...] = y

    @pl.when(f > 0)
    def _():
        acc[...] += y

    @pl.when(f == pl.num_programs(2) - 1)
    def _():
        o_ref[...] = acc[...].astype(BF16)


def _expert_ffn(xg, w_gate, w_up, w_down, tf):
    b, e, cap, _ = xg.shape
    return pl.pallas_call(
        _ffn_body,
        grid=(b, e, EXPERT_FF // tf),
        in_specs=[pl.BlockSpec((None, None, cap, D_MODEL), lambda bi, ei, f: (bi, ei, 0, 0)),
                  pl.BlockSpec((None, D_MODEL, tf), lambda bi, ei, f: (ei, 0, f)),
                  pl.BlockSpec((None, D_MODEL, tf), lambda bi, ei, f: (ei, 0, f)),
                  pl.BlockSpec((None, tf, D_MODEL), lambda bi, ei, f: (ei, f, 0))],
        out_specs=pl.BlockSpec((None, None, cap, D_MODEL), lambda bi, ei, f: (bi, ei, 0, 0)),
        out_shape=SDS((b, e, cap, D_MODEL), BF16),
        scratch_shapes=[pltpu.VMEM((cap, D_MODEL), F32)],
        compiler_params=_cparams(("parallel", "parallel", "arbitrary")),
        name="expert_ffn",
    )(xg, w_gate, w_up, w_down)


def _combine_body(meta_ref, x_ref, posm_ref, gate_ref, y_ref, o_ref, *, vc, ts):
    bi, vi = pl.program_id(0), pl.program_id(1)
    base = (bi * vc + vi) * 5
    tile = meta_ref[base + 2]

    @pl.when(meta_ref[base + 3] == 1)
    def _():
        o_ref[...] = x_ref[...]

    @pl.when(meta_ref[base + 4] == 1)
    def _():
        pos = posm_ref[...]
        slot = tile * ts + lax.broadcasted_iota(I32, (ts, pos.shape[1]), 0)
        w = jnp.where(pos == slot, gate_ref[...], 0.0).astype(BF16)
        o_ref[...] += lax.dot_general(w, y_ref[...], TN_DIMS, preferred_element_type=F32)


def _combine(c_meta, vc, x3, posm4, gate4, y4, ts, tk):
    b, s, _ = x3.shape

    def meta_at(bi, vi, k):
        return (bi * vc + vi) * 5 + k

    body = functools.partial(_combine_body, vc=vc, ts=ts)
    route_spec = pl.BlockSpec((None, None, 1, tk),
                              lambda bi, vi, m: (bi, m[meta_at(bi, vi, 1)], 0, m[meta_at(bi, vi, 0)]))
    grid_spec = pltpu.PrefetchScalarGridSpec(
        num_scalar_prefetch=1,
        grid=(b, vc),
        in_specs=[pl.BlockSpec((None, tk, D_MODEL), lambda bi, vi, m: (bi, m[meta_at(bi, vi, 0)], 0)),
                  route_spec, route_spec,
                  pl.BlockSpec((None, None, ts, D_MODEL),
                               lambda bi, vi, m: (bi, m[meta_at(bi, vi, 1)], m[meta_at(bi, vi, 2)], 0))],
        out_specs=pl.BlockSpec((None, tk, D_MODEL), lambda bi, vi, m: (bi, m[meta_at(bi, vi, 0)], 0)),
    )
    return pl.pallas_call(
        body,
        grid_spec=grid_spec,
        out_shape=SDS((b, s, D_MODEL), F32),
        compiler_params=_cparams(("parallel", "arbitrary")),
        name="moe_combine",
    )(c_meta, x3, posm4, gate4, y4)


def _repack_w_in(w_in):
    na_w = NA_HEADS * HEAD_DIM
    o = np.cumsum([0, na_w, na_w, na_w, 512, 128, 128, 512, 1024, 16, 1024])
    naq, nak, nav, waq, wak, wav, z, xbc, dt, conf = [w_in[..., o[i]:o[i + 1]] for i in range(10)]
    w_main = jnp.concatenate([xbc, conf, naq, nak, nav, waq, z, wak, wav], axis=-1).astype(BF16)
    w_dt = jnp.pad(dt, ((0, 0), (0, 0), (0, LANES - dt.shape[-1]))).astype(BF16)
    return w_main, w_dt


def _pad_lanes(v, width=LANES):
    return jnp.pad(v, [(0, 0)] * (v.ndim - 1) + [(0, width - v.shape[-1])])


def kernel(x, mix_norm_w, w_in, na_q_norm, na_k_norm, na_rpb, wa_q_norm, wa_k_norm, wa_sink, ssm_conv_w, ssm_conv_b, ssm_dt_bias, ssm_a_log, ssm_d, ssm_norm_w, conf_dw_w, conf_dw_b, conf_ln_w, conf_ln_b, w_out, ffn_norm_w, w_router, w_gate, w_up, w_down):
    b, s, d = x.shape
    depth = w_in.shape[0]
    rows = s // GRID_W
    cap = EC_CAPACITY * s // N_EXPERTS
    r_blk, w_blk = 4, 12
    ts = min(256, cap)
    tk = min(512, s)

    w_main, w_dt = _repack_w_in(w_in)
    w_out4 = w_out.astype(BF16).reshape(depth, 4, 512, D_MODEL)
    wr_t = jnp.swapaxes(w_router, 1, 2)
    wr_hi = wr_t.astype(BF16)
    wr_lo = (wr_t - wr_hi.astype(F32)).astype(BF16)
    cos_t, sin_t = _rope_tables(s)
    bd = jnp.asarray(np.kron(np.eye(LANES // HEAD_DIM), np.full((HEAD_DIM, HEAD_DIM), 1.0 / HEAD_DIM)), BF16)
    eh_np = np.zeros((2, LANES, SSM_HEADS * HEAD_DIM), np.float32)
    for dr in range(2):
        for h in range(SSM_HEADS):
            eh_np[dr, dr * SSM_HEADS + h, h * HEAD_DIM:(h + 1) * HEAD_DIM] = 1.0
    eh = jnp.asarray(eh_np, BF16)
    tile2 = lambda v: jnp.tile(v, (1, LANES // HEAD_DIM))[:, None, :]
    naq_w, nak_w, waq_w, wak_w = tile2(na_q_norm), tile2(na_k_norm), tile2(wa_q_norm), tile2(wa_k_norm)
    dtb = _pad_lanes(ssm_dt_bias.reshape(depth, 1, 2 * SSM_HEADS))
    a_vec = _pad_lanes(-jnp.exp(ssm_a_log.reshape(depth, 1, 2 * SSM_HEADS)))
    dvec = jnp.repeat(ssm_d, HEAD_DIM, axis=-1)[:, None, :]
    conv_w = jnp.pad(ssm_conv_w, ((0, 0), (0, 8 - SSM_CONV), (0, 0)))
    conf_w = jnp.pad(conf_dw_w, ((0, 0), (0, 32 - CONF_KERNEL), (0, 0)))

    for l in range(depth):
        u, dt_raw = _in_proj(x.reshape(b * s, d), mix_norm_w[l][None], w_main[l], w_dt[l])
        u3 = u.reshape(b, s, U_WIDTH)
        dt3 = dt_raw.reshape(b, s, LANES)
        bias_tab = _na_bias_table(na_rpb[l], r_blk, w_blk, rows)
        o_a = _na_attention(u3, naq_w[l], nak_w[l], bias_tab, bd, r_blk, w_blk)
        o_b = _wa_attention(u3, wa_sink[l][None], cos_t, sin_t, waq_w[l], wak_w[l], bd, min(256, s))
        ssd_args = (u3, dt3, conv_w[l], ssm_conv_b[l][None], dtb[l], a_vec[l])
        y_f = _ssd_pass(*ssd_args, eh[0], None, reverse=False)
        o_c = _ssd_pass(*ssd_args, eh[1], (y_f, dvec[l], ssm_norm_w[l][None]), reverse=True)
        o_d = _conformer(u3, conf_w[l], conf_dw_b[l][None], conf_ln_w[l][None], conf_ln_b[l][None], min(256, s))
        x, h2, logits_t = _out_proj(o_a, o_b, o_c, o_d, w_out4[l], x, ffn_norm_w[l][None], wr_hi[l], wr_lo[l],
                                    min(256, s))
        posm, gate, rng = _route(logits_t, cap, ts)
        g_meta, vg, c_meta, vc = _visit_tables(rng, s, cap, ts, tk)
        posm4 = posm.reshape(b, N_EXPERTS, 1, s)
        gate4 = gate.reshape(b, N_EXPERTS, 1, s)
        xg = _gather(g_meta, vg, posm4, h2, cap, ts, tk)
        y4 = _expert_ffn(xg, w_gate[l], w_up[l], w_down[l], 256)
        x = _combine(c_meta, vc, x, posm4, gate4, y4, ts, tk)
    return x
```

```python
import functools

import numpy as np
import jax
import jax.numpy as jnp
from jax import lax
from jax.experimental import pallas as pl
from jax.experimental.pallas import tpu as pltpu

F32 = jnp.float32
BF16 = jnp.bfloat16
I32 = jnp.int32
SDS = jax.ShapeDtypeStruct

D_MODEL = 2048
HEAD_DIM = 64
EPS = 1e-6
NA_HEADS = 8
NA_KH = 8
NA_KW = 16
GRID_W = 64
WA_Q_HEADS = 8
WA_KV_HEADS = 2
WINDOW = 128
ROPE_THETA = 500000.0
ROPE_DIM = HEAD_DIM // 4
SSM_HEADS = 8
SSM_STATE = 128
SSM_CONV = 5
SSM_CHUNK = 128
CONF_WIDTH = 512
CONF_KERNEL = 31
N_EXPERTS = 16
EC_CAPACITY = 2
EXPERT_FF = D_MODEL // 2

U_XBC = 0
U_CONF = 1024
U_NAQ = 2048
U_NAK = 2560
U_NAV = 3072
U_WAQ = 3584
U_Z = 4096
U_WAK = 4608
U_WAV = 4736
U_WIDTH = 4864
LANES = 128
SLOT_ALIGN = 16
VMEM_LIMIT = 56 * 1024 * 1024

NT_DIMS = (((1,), (1,)), ((), ()))
TN_DIMS = (((0,), (0,)), ((), ()))
NEG = -1e30


def _cparams(sem):
    return pltpu.CompilerParams(dimension_semantics=sem, vmem_limit_bytes=VMEM_LIMIT)


def _split3(v):
    a = v.astype(BF16)
    r = v - a.astype(F32)
    b = r.astype(BF16)
    c = (r - b.astype(F32)).astype(BF16)
    return a, b, c


def _sigmoid(v):
    return 1.0 / (1.0 + jnp.exp(-v))


def _head_rms(xf, w, bd):
    sq = xf * xf
    hi = sq.astype(BF16)
    lo = (sq - hi.astype(F32)).astype(BF16)
    ms = (jnp.dot(hi, bd, preferred_element_type=F32)
          + jnp.dot(lo, bd, preferred_element_type=F32))
    return xf * lax.rsqrt(ms + EPS) * w


def _inproj_body(x_ref, nw_ref, w_ref, wdt_ref, u_ref, dt_ref, h_scr):
    @pl.when(pl.program_id(1) == 0)
    def _():
        x = x_ref[...]
        ms = jnp.mean(x * x, axis=-1, keepdims=True)
        h = (x * lax.rsqrt(ms + EPS) * nw_ref[...]).astype(BF16)
        h_scr[...] = h
        dt_ref[...] = jnp.dot(h, wdt_ref[...], preferred_element_type=F32)

    u_ref[...] = jnp.dot(h_scr[...], w_ref[...], preferred_element_type=F32).astype(BF16)


def _in_proj(x2, norm_w, w_main, w_dt, tm=512):
    m = x2.shape[0]
    tn = U_WIDTH // 2
    return pl.pallas_call(
        _inproj_body,
        grid=(m // tm, U_WIDTH // tn),
        in_specs=[pl.BlockSpec((tm, D_MODEL), lambda i, j: (i, 0)),
                  pl.BlockSpec((1, D_MODEL), lambda i, j: (0, 0)),
                  pl.BlockSpec((D_MODEL, tn), lambda i, j: (0, j)),
                  pl.BlockSpec((D_MODEL, LANES), lambda i, j: (0, 0))],
        out_specs=[pl.BlockSpec((tm, tn), lambda i, j: (i, j)),
                   pl.BlockSpec((tm, LANES), lambda i, j: (i, 0))],
        out_shape=[SDS((m, U_WIDTH), BF16), SDS((m, LANES), F32)],
        scratch_shapes=[pltpu.VMEM((tm, D_MODEL), BF16)],
        compiler_params=_cparams(("parallel", "arbitrary")),
        name="in_proj",
    )(x2, norm_w, w_main, w_dt)


def _na_body(q_ref, k_ref, v_ref, qw_ref, kw_ref, bias_ref, bd_ref, o_ref, kn_scr, *, rq, wk, rows, r_blk, w_blk):
    s = q_ref.shape[0]
    nrb = rows // r_blk
    bd = bd_ref[...]
    ch = min(512, s)

    def kprep(i, c):
        s0 = pl.multiple_of(i * ch, ch)
        kf = k_ref[pl.ds(s0, ch), :].astype(F32)
        kn_scr[pl.ds(s0, ch), :] = _head_rms(kf, kw_ref[...], bd).astype(BF16)
        return c

    lax.fori_loop(0, s // ch, kprep, 0)
    lane = lax.broadcasted_iota(I32, (rq, LANES), 1)

    def blk(rb, c):
        q0 = pl.multiple_of(rb * rq, rq)
        w0 = pl.multiple_of(jnp.clip(rb * r_blk - NA_KH // 2, 0, rows - w_blk) * GRID_W, GRID_W)
        cls = jnp.where(rb == 0, 0, jnp.where(rb == nrb - 1, 2, 1))
        qn = _head_rms(q_ref[pl.ds(q0, rq), :].astype(F32), qw_ref[...], bd) * (HEAD_DIM ** -0.5)
        kw = kn_scr[pl.ds(w0, wk), :]
        vw = v_ref[pl.ds(w0, wk), :]
        outs = []
        for h in range(2):
            qh = jnp.where((lane // HEAD_DIM) == h, qn, 0.0).astype(BF16)
            lg = lax.dot_general(qh, kw, NT_DIMS, preferred_element_type=F32)
            lg = lg + bias_ref[h, cls].astype(F32)
            m = jnp.max(lg, axis=-1, keepdims=True)
            p = jnp.exp(lg - m)
            den = jnp.sum(p, axis=-1, keepdims=True)
            outs.append(jnp.dot(p.astype(BF16), vw, preferred_element_type=F32) * (1.0 / den))
        o_ref[pl.ds(q0, rq), :] = jnp.where(lane < HEAD_DIM, outs[0], outs[1]).astype(BF16)
        return c

    lax.fori_loop(0, nrb, blk, 0)


def _na_bias_table(rpb, r_blk, w_blk, rows):
    depth = rpb.shape[0]
    nrb = rows // r_blk
    col = np.arange(GRID_W)
    cstart = np.clip(col - NA_KW // 2, 0, GRID_W - NA_KW)
    valid_col = (col[None, :] >= cstart[:, None]) & (col[None, :] < cstart[:, None] + NA_KW)
    bj = np.clip(col[None, :] - col[:, None] + NA_KW - 1, 0, 2 * NA_KW - 2)
    sel_col = (bj[:, :, None] == np.arange(2 * NA_KW - 1)) & valid_col[:, :, None]
    sel_row, valid_rows = [], []
    for rb in (0, min(1, nrb - 1), nrb - 1):
        r = rb * r_blk + np.arange(r_blk)
        w0 = np.clip(rb * r_blk - NA_KH // 2, 0, rows - w_blk)
        kr = w0 + np.arange(w_blk)
        r0 = np.clip(r - NA_KH // 2, 0, rows - NA_KH)
        valid_row = (kr[None, :] >= r0[:, None]) & (kr[None, :] < r0[:, None] + NA_KH)
        bi = np.clip(kr[None, :] - r[:, None] + NA_KH - 1, 0, 2 * NA_KH - 2)
        sel_row.append((bi[:, :, None] == np.arange(2 * NA_KH - 1)) & valid_row[:, :, None])
        valid_rows.append(valid_row)
    sel_row = jnp.asarray(np.stack(sel_row), F32)
    mask = np.stack(valid_rows)[:, :, None, :, None] & valid_col[None, None, :, None, :]
    t = jnp.einsum('lhij,ckj->lhick', rpb, jnp.asarray(sel_col, F32), precision=lax.Precision.HIGHEST)
    t = jnp.einsum('lhick,grwi->lhgrcwk', t, sel_row, precision=lax.Precision.HIGHEST)
    t = jnp.where(jnp.asarray(mask)[None, None], t, NEG)
    return t.reshape(depth, NA_HEADS // 2, 2, 3, r_blk * GRID_W, w_blk * GRID_W).astype(BF16)


def _na_attention(u3, qw, kw, bias_tab, bd, r_blk, w_blk):
    b, s, _ = u3.shape
    rows = s // GRID_W
    rq, wk = r_blk * GRID_W, w_blk * GRID_W
    body = functools.partial(_na_body, rq=rq, wk=wk, rows=rows, r_blk=r_blk, w_blk=w_blk)
    return pl.pallas_call(
        body,
        grid=(b, NA_HEADS // 2),
        in_specs=[pl.BlockSpec((None, s, LANES), lambda i, h: (i, 0, U_NAQ // LANES + h)),
                  pl.BlockSpec((None, s, LANES), lambda i, h: (i, 0, U_NAK // LANES + h)),
                  pl.BlockSpec((None, s, LANES), lambda i, h: (i, 0, U_NAV // LANES + h)),
                  pl.BlockSpec((1, LANES), lambda i, h: (0, 0)),
                  pl.BlockSpec((1, LANES), lambda i, h: (0, 0)),
                  pl.BlockSpec((None, 2, 3, rq, wk), lambda i, h: (h, 0, 0, 0, 0)),
                  pl.BlockSpec((LANES, LANES), lambda i, h: (0, 0))],
        out_specs=pl.BlockSpec((None, s, LANES), lambda i, h: (i, 0, h)),
        out_shape=SDS((b, s, NA_HEADS * HEAD_DIM), BF16),
        scratch_shapes=[pltpu.VMEM((s, LANES), BF16)],
        compiler_params=_cparams(("parallel", "parallel")),
        name="na2d",
    )(u3, u3, u3, qw, kw, bias_tab, bd)


def _rope(xf, cos, sin):
    lane = lax.broadcasted_iota(I32, xf.shape, 1)
    half = ROPE_DIM // 2
    partner = jnp.where((lane % HEAD_DIM) < half, pltpu.roll(xf, LANES - half, 1), pltpu.roll(xf, half, 1))
    return xf * cos + partner * sin


def _wa_body(sink_ref, q_ref, k_ref, v_ref, cos_ref, sin_ref, qw_ref, kw_ref, bd_ref, o_ref, kd_scr, vd_scr,
             *, qb, wkb):
    s = k_ref.shape[0]
    kvh = pl.program_id(1)
    n = pl.program_id(2)
    bd = bd_ref[...]
    ch = min(512, s)

    @pl.when(n == 0)
    def _():
        lane = lax.broadcasted_iota(I32, (ch, LANES), 1)
        mine = (lane // HEAD_DIM) == kvh

        def prep(i, c):
            s0 = pl.multiple_of(i * ch, ch)
            kf = k_ref[pl.ds(s0, ch), :].astype(F32)
            kf = _rope(_head_rms(kf, kw_ref[...], bd), cos_ref[pl.ds(s0, ch), :], sin_ref[pl.ds(s0, ch), :])
            kd_scr[pl.ds(s0, ch), :] = jnp.where(mine, kf, pltpu.roll(kf, HEAD_DIM, 1)).astype(BF16)
            vf = v_ref[pl.ds(s0, ch), :].astype(F32)
            vd_scr[pl.ds(s0, ch), :] = jnp.where(mine, vf, pltpu.roll(vf, HEAD_DIM, 1)).astype(BF16)
            return c

        lax.fori_loop(0, s // ch, prep, 0)

    q0 = pl.multiple_of(n * qb, qb)
    ks = pl.multiple_of(jnp.clip(n * qb - WINDOW, 0, s - wkb), LANES)
    cosq = cos_ref[pl.ds(q0, qb), :]
    sinq = sin_ref[pl.ds(q0, qb), :]
    kw = kd_scr[pl.ds(ks, wkb), :]
    vw = vd_scr[pl.ds(ks, wkb), :]
    qpos = q0 + lax.broadcasted_iota(I32, (qb, wkb), 0)
    kpos = ks + lax.broadcasted_iota(I32, (qb, wkb), 1)
    allowed = jnp.abs(kpos - qpos) <= WINDOW
    lane = lax.broadcasted_iota(I32, (qb, LANES), 1)
    for half in range(2):
        qf = q_ref[:, half * LANES:(half + 1) * LANES].astype(F32)
        qn = _rope(_head_rms(qf, qw_ref[...], bd), cosq, sinq) * (HEAD_DIM ** -0.5)
        outs = []
        for gg in range(2):
            qh = jnp.where((lane // HEAD_DIM) == gg, qn, 0.0).astype(BF16)
            lg = lax.dot_general(qh, kw, NT_DIMS, preferred_element_type=F32)
            lg = jnp.where(allowed, lg, NEG)
            sk = sink_ref[0, kvh * 4 + half * 2 + gg]
            m = jnp.maximum(jnp.max(lg, axis=-1, keepdims=True), sk)
            p = jnp.exp(lg - m)
            den = jnp.sum(p, axis=-1, keepdims=True) + jnp.exp(sk - m)
            outs.append(jnp.dot(p.astype(BF16), vw, preferred_element_type=F32) * (1.0 / den))
        o_ref[:, half * LANES:(half + 1) * LANES] = jnp.where(lane < HEAD_DIM, outs[0], outs[1]).astype(BF16)


def _wa_attention(u3, sink, cos_t, sin_t, qw, kw, bd, qb):
    b, s, _ = u3.shape
    wkb = qb + 2 * WINDOW
    body = functools.partial(_wa_body, qb=qb, wkb=wkb)
    return pl.pallas_call(
        body,
        grid=(b, WA_KV_HEADS, s // qb),
        in_specs=[pl.BlockSpec(memory_space=pltpu.SMEM),
                  pl.BlockSpec((None, qb, 2 * LANES), lambda i, h, n: (i, n, U_WAQ // (2 * LANES) + h)),
                  pl.BlockSpec((None, s, LANES), lambda i, h, n: (i, 0, U_WAK // LANES)),
                  pl.BlockSpec((None, s, LANES), lambda i, h, n: (i, 0, U_WAV // LANES)),
                  pl.BlockSpec((s, LANES), lambda i, h, n: (0, 0)),
                  pl.BlockSpec((s, LANES), lambda i, h, n: (0, 0)),
                  pl.BlockSpec((1, LANES), lambda i, h, n: (0, 0)),
                  pl.BlockSpec((1, LANES), lambda i, h, n: (0, 0)),
                  pl.BlockSpec((LANES, LANES), lambda i, h, n: (0, 0))],
        out_specs=pl.BlockSpec((None, qb, 2 * LANES), lambda i, h, n: (i, n, h)),
        out_shape=SDS((b, s, WA_Q_HEADS * HEAD_DIM), BF16),
        scratch_shapes=[pltpu.VMEM((s, LANES), BF16), pltpu.VMEM((s, LANES), BF16)],
        compiler_params=_cparams(("parallel", "arbitrary", "arbitrary")),
        name="swa_gqa",
    )(sink, u3, u3, u3, cos_t, sin_t, qw, kw, bd)


def _rope_tables(s):
    half = ROPE_DIM // 2
    inv_freq = 1.0 / (ROPE_THETA ** (jnp.arange(half, dtype=F32) * 2.0 / ROPE_DIM))
    ang = jnp.arange(s, dtype=F32)[:, None] * inv_freq[None, :]
    cos, sin = jnp.cos(ang), jnp.sin(ang)
    ones = jnp.ones((s, HEAD_DIM - ROPE_DIM), F32)
    cos_h = jnp.concatenate([cos, cos, ones], axis=1)
    sin_h = jnp.concatenate([-sin, sin, 0.0 * ones], axis=1)
    return jnp.tile(cos_h, (1, LANES // HEAD_DIM)), jnp.tile(sin_h, (1, LANES // HEAD_DIM))


def _ssd_body(*refs, reverse, nc, final):
    if final:
        (xp_ref, xc_ref, xn_ref, dt_ref, cw_ref, cb_ref, dtb_ref, a_ref, eh_ref,
         yf_ref, z_ref, dv_ref, nw_ref, o_ref, st_scr, cv_scr) = refs
    else:
        (xp_ref, xc_ref, xn_ref, dt_ref, cw_ref, cb_ref, dtb_ref, a_ref, eh_ref,
         o_ref, st_scr, cv_scr) = refs
    ci = pl.program_id(1)
    c = (nc - 1 - ci) if reverse else ci
    L = SSM_CHUNK
    sw = SSM_HEADS * HEAD_DIM

    @pl.when(ci == 0)
    def _():
        st_scr[...] = jnp.zeros_like(st_scr)

    has_prev = jnp.where(c > 0, 1.0, 0.0)
    has_next = jnp.where(c < nc - 1, 1.0, 0.0)
    cv_scr[0:8, :] = xp_ref[L - 8:L, :].astype(F32) * has_prev
    cv_scr[8:8 + L, :] = xc_ref[...].astype(F32)
    cv_scr[8 + L:16 + L, :] = xn_ref[0:8, :].astype(F32) * has_next
    left = (SSM_CONV - 1) // 2
    acc = cb_ref[...] + cw_ref[0:1, :] * cv_scr[pl.ds(8 - left, L), :]
    for k in range(1, SSM_CONV):
        acc = acc + cw_ref[k:k + 1, :] * cv_scr[pl.ds(8 - left + k, L), :]
    xa = acc * _sigmoid(acc)
    xs = xa[:, 0:sw]

    raw = dt_ref[...] + dtb_ref[...]
    dtv = jnp.maximum(raw, 0.0) + jnp.log1p(jnp.exp(-jnp.abs(raw)))
    da = dtv * a_ref[...]
    ri = lax.broadcasted_iota(I32, (L, L), 0)
    cj = lax.broadcasted_iota(I32, (L, L), 1)
    keep = (cj >= ri) if reverse else (cj <= ri)
    tri = jnp.where(keep, 1.0, 0.0).astype(BF16)
    d1, d2, d3 = _split3(da)
    cum = (jnp.dot(tri, d1, preferred_element_type=F32) + jnp.dot(tri, d2, preferred_element_type=F32)
           + jnp.dot(tri, d3, preferred_element_type=F32))
    cum_t = cum.T
    eh = eh_ref[...]
    c1, c2, c3 = _split3(cum)
    cum_x = (jnp.dot(c1, eh, preferred_element_type=F32) + jnp.dot(c2, eh, preferred_element_type=F32)
             + jnp.dot(c3, eh, preferred_element_type=F32))
    t1, t2, t3 = _split3(dtv)
    dt_x = (jnp.dot(t1, eh, preferred_element_type=F32) + jnp.dot(t2, eh, preferred_element_type=F32)
            + jnp.dot(t3, eh, preferred_element_type=F32))
    tot_x = cum_x[0:1, :] if reverse else cum_x[L - 1:L, :]
    xc = xs * dt_x
    xdec = (xc * jnp.exp(tot_x - cum_x)).astype(BF16)
    xcb = xc.astype(BF16)
    ecum = jnp.exp(cum_x)
    etot = jnp.exp(tot_x)
    lane = lax.broadcasted_iota(I32, (L, LANES), 1)
    doff = SSM_HEADS if reverse else 0
    for g in range(2):
        bg = xa[:, sw + g * SSM_STATE: sw + (g + 1) * SSM_STATE]
        cg = xa[:, sw + 2 * SSM_STATE + g * SSM_STATE: sw + 2 * SSM_STATE + (g + 1) * SSM_STATE]
        cgb = cg.astype(BF16)
        cb = lax.dot_general(cgb, bg.astype(BF16), NT_DIMS, preferred_element_type=F32)
        bgt = bg.T.astype(BF16)
        for pp in range(2):
            p = g * 2 + pp
            sl = slice(p * LANES, (p + 1) * LANES)
            ys = []
            for hh in range(2):
                j = doff + 2 * p + hh
                seg = cum[:, j:j + 1] - cum_t[j:j + 1, :]
                dec = jnp.exp(jnp.where(keep, seg, NEG))
                ys.append(jnp.dot((cb * dec).astype(BF16), xcb[:, sl], preferred_element_type=F32))
            y = jnp.where(lane < HEAD_DIM, ys[0], ys[1])
            prev = st_scr[p]
            y = y + jnp.dot(cgb, prev.astype(BF16), preferred_element_type=F32) * ecum[:, sl]
            st_scr[p] = etot[:, sl] * prev + jnp.dot(bgt, xdec[:, sl], preferred_element_type=F32)
            if final:
                y = y + yf_ref[:, sl] + xs[:, sl] * dv_ref[:, sl]
                zf = z_ref[:, sl].astype(F32)
                cv_scr[0:L, sl] = y * (zf * _sigmoid(zf))
            else:
                o_ref[:, sl] = y
    if final:
        gated = cv_scr[0:L, 0:sw]
        ms = jnp.mean(gated * gated, axis=-1, keepdims=True)
        o_ref[...] = (gated * lax.rsqrt(ms + EPS) * nw_ref[...]).astype(BF16)


def _ssd_pass(u3, dt3, cw, cb, dtb, a_vec, eh, extra, reverse):
    b, s, _ = u3.shape
    L = SSM_CHUNK
    nc = s // L
    sw = SSM_HEADS * HEAD_DIM
    final = extra is not None

    def cidx(ci):
        return (nc - 1 - ci) if reverse else ci

    xw = 2 * sw
    in_specs = [pl.BlockSpec((None, L, xw), lambda i, ci: (i, jnp.maximum(cidx(ci) - 1, 0), U_XBC // xw)),
                pl.BlockSpec((None, L, xw), lambda i, ci: (i, cidx(ci), U_XBC // xw)),
                pl.BlockSpec((None, L, xw), lambda i, ci: (i, jnp.minimum(cidx(ci) + 1, nc - 1), U_XBC // xw)),
                pl.BlockSpec((None, L, LANES), lambda i, ci: (i, cidx(ci), 0)),
                pl.BlockSpec((8, xw), lambda i, ci: (0, 0)),
                pl.BlockSpec((1, xw), lambda i, ci: (0, 0)),
                pl.BlockSpec((1, LANES), lambda i, ci: (0, 0)),
                pl.BlockSpec((1, LANES), lambda i, ci: (0, 0)),
                pl.BlockSpec((LANES, sw), lambda i, ci: (0, 0))]
    args = [u3, u3, u3, dt3, cw, cb, dtb, a_vec, eh]
    if final:
        y_f, dvec, nw = extra
        in_specs += [pl.BlockSpec((None, L, sw), lambda i, ci: (i, cidx(ci), 0)),
                     pl.BlockSpec((None, L, sw), lambda i, ci: (i, cidx(ci), U_Z // sw)),
                     pl.BlockSpec((1, sw), lambda i, ci: (0, 0)),
                     pl.BlockSpec((1, sw), lambda i, ci: (0, 0))]
        args += [y_f, u3, dvec, nw]
    body = functools.partial(_ssd_body, reverse=reverse, nc=nc, final=final)
    return pl.pallas_call(
        body,
        grid=(b, nc),
        in_specs=in_specs,
        out_specs=pl.BlockSpec((None, L, sw), lambda i, ci: (i, cidx(ci), 0)),
        out_shape=SDS((b, s, sw), BF16 if final else F32),
        scratch_shapes=[pltpu.VMEM((4, SSM_STATE, LANES), F32), pltpu.VMEM((L + 16, xw), F32)],
        compiler_params=_cparams(("parallel", "arbitrary")),
        name="ssd_bwd" if reverse else "ssd_fwd",
    )(*args)


def _conf_body(xp_ref, xc_ref, xn_ref, w_ref, b_ref, lnw_ref, lnb_ref, o_ref, scr, *, tc, nt):
    i = pl.program_id(1)
    cwid = CONF_WIDTH
    halo = 16

    def glu(v):
        return v[:, 0:cwid].astype(F32) * _sigmoid(v[:, cwid:2 * cwid].astype(F32))

    scr[0:halo, :] = glu(xp_ref[tc - halo:tc, :]) * jnp.where(i > 0, 1.0, 0.0)
    scr[halo:halo + tc, :] = glu(xc_ref[...])
    scr[halo + tc:2 * halo + tc, :] = glu(xn_ref[0:halo, :]) * jnp.where(i < nt - 1, 1.0, 0.0)
    left = (CONF_KERNEL - 1) // 2
    acc = b_ref[...] + w_ref[0:1, :] * scr[pl.ds(halo - left, tc), :]
    for k in range(1, CONF_KERNEL):
        acc = acc + w_ref[k:k + 1, :] * scr[pl.ds(halo - left + k, tc), :]
    mu = jnp.mean(acc, axis=-1, keepdims=True)
    cen = acc - mu
    var = jnp.mean(cen * cen, axis=-1, keepdims=True)
    y = cen * lax.rsqrt(var + EPS) * lnw_ref[...] + lnb_ref[...]
    o_ref[...] = (y * _sigmoid(y)).astype(BF16)


def _conformer(u3, w, bvec, lnw, lnb, tc):
    b, s, _ = u3.shape
    nt = s // tc
    cw2 = 2 * CONF_WIDTH
    body = functools.partial(_conf_body, tc=tc, nt=nt)
    return pl.pallas_call(
        body,
        grid=(b, nt),
        in_specs=[pl.BlockSpec((None, tc, cw2), lambda i, t: (i, jnp.maximum(t - 1, 0), U_CONF // cw2)),
                  pl.BlockSpec((None, tc, cw2), lambda i, t: (i, t, U_CONF // cw2)),
                  pl.BlockSpec((None, tc, cw2), lambda i, t: (i, jnp.minimum(t + 1, nt - 1), U_CONF // cw2)),
                  pl.BlockSpec((32, CONF_WIDTH), lambda i, t: (0, 0)),
                  pl.BlockSpec((1, CONF_WIDTH), lambda i, t: (0, 0)),
                  pl.BlockSpec((1, CONF_WIDTH), lambda i, t: (0, 0)),
                  pl.BlockSpec((1, CONF_WIDTH), lambda i, t: (0, 0))],
        out_specs=pl.BlockSpec((None, tc, CONF_WIDTH), lambda i, t: (i, t, 0)),
        out_shape=SDS((b, s, CONF_WIDTH), BF16),
        scratch_shapes=[pltpu.VMEM((tc + 32, CONF_WIDTH), F32)],
        compiler_params=_cparams(("parallel", "parallel")),
        name="conformer",
    )(u3, u3, u3, w, bvec, lnw, lnb)


def _outproj_body(oa_ref, ob_ref, oc_ref, od_ref, w_ref, x_ref, fw_ref, wrh_ref, wrl_ref, xo_ref, h_ref, lg_ref):
    acc = x_ref[...]
    for g, r in enumerate((oa_ref, ob_ref, oc_ref, od_ref)):
        acc = acc + jnp.dot(r[...], w_ref[g], preferred_element_type=F32)
    xo_ref[...] = acc
    ms = jnp.mean(acc * acc, axis=-1, keepdims=True)
    hf = acc * lax.rsqrt(ms + EPS) * fw_ref[...]
    hb = hf.astype(BF16)
    h_ref[...] = hb
    lo = (hf - hb.astype(F32)).astype(BF16)
    wrh = wrh_ref[...]
    lg = (jnp.dot(hb, wrh, preferred_element_type=F32) + jnp.dot(lo, wrh, preferred_element_type=F32)
          + jnp.dot(hb, wrl_ref[...], preferred_element_type=F32))
    lg_ref[...] = lg.T[0:N_EXPERTS, :]


def _out_proj(o_a, o_b, o_c, o_d, w4, x3, fw, wrh, wrl, tm):
    b, s, _ = x3.shape
    gw = 512
    mix_spec = pl.BlockSpec((None, tm, gw), lambda i, t: (i, t, 0))
    return pl.pallas_call(
        _outproj_body,
        grid=(b, s // tm),
        in_specs=[mix_spec, mix_spec, mix_spec, mix_spec,
                  pl.BlockSpec((4, gw, D_MODEL), lambda i, t: (0, 0, 0)),
                  pl.BlockSpec((None, tm, D_MODEL), lambda i, t: (i, t, 0)),
                  pl.BlockSpec((1, D_MODEL), lambda i, t: (0, 0)),
                  pl.BlockSpec((D_MODEL, LANES), lambda i, t: (0, 0)),
                  pl.BlockSpec((D_MODEL, LANES), lambda i, t: (0, 0))],
        out_specs=[pl.BlockSpec((None, tm, D_MODEL), lambda i, t: (i, t, 0)),
                   pl.BlockSpec((None, tm, D_MODEL), lambda i, t: (i, t, 0)),
                   pl.BlockSpec((None, N_EXPERTS, tm), lambda i, t: (i, 0, t))],
        out_shape=[SDS((b, s, D_MODEL), F32), SDS((b, s, D_MODEL), BF16), SDS((b, N_EXPERTS, s), F32)],
        compiler_params=_cparams(("parallel", "parallel")),
        name="out_proj",
    )(o_a, o_b, o_c, o_d, w4, x3, fw, wrh, wrl)


def _route_body(lg_ref, posm_ref, gate_ref, offs_ref, m_scr, c_scr, *, cap, tk):
    e, s = lg_ref.shape
    cb = 256
    nblk = s // cb
    lg = lg_ref[...]
    mx = jnp.max(lg, axis=0, keepdims=True)
    ex = jnp.exp(lg - mx)
    aff = ex / jnp.sum(ex, axis=0, keepdims=True)
    gate_ref[...] = aff
    bits = pltpu.bitcast(aff, I32)

    def search(i, v):
        cand = v | jnp.left_shift(jnp.int32(1), 30 - i)
        cnt = jnp.sum(jnp.where(bits >= cand, 1.0, 0.0), axis=1, keepdims=True)
        return jnp.where(cnt >= cap, cand, v)

    thr = lax.fori_loop(0, 31, search, jnp.zeros((e, 1), I32))
    gt = bits > thr
    eq = bits == thr
    need = cap - jnp.sum(jnp.where(gt, 1.0, 0.0), axis=1, keepdims=True)
    ri = lax.broadcasted_iota(I32, (cb, cb), 0)
    cj = lax.broadcasted_iota(I32, (cb, cb), 1)
    tri = jnp.where(ri <= cj, 1.0, 0.0).astype(BF16)

    def cumsum_into_c():
        def blk(i, carry):
            o = pl.multiple_of(i * cb, cb)
            inc = jnp.dot(m_scr[:, pl.ds(o, cb)].astype(BF16), tri, preferred_element_type=F32) + carry
            c_scr[:, pl.ds(o, cb)] = inc
            return inc[:, cb - 1:cb]
        lax.fori_loop(0, nblk, blk, jnp.zeros((e, 1), F32))

    eqf = jnp.where(eq, 1.0, 0.0)
    m_scr[...] = eqf
    cumsum_into_c()
    sel = gt | (eq & ((c_scr[...] - eqf) < need))
    self_f = jnp.where(sel, 1.0, 0.0)
    m_scr[...] = self_f
    cumsum_into_c()
    posm_ref[...] = jnp.where(sel, (c_scr[...] - self_f).astype(I32), -1)
    lane = lax.broadcasted_iota(I32, (e, LANES), 1)
    offs = jnp.zeros((e, LANES), I32)
    for c in range(1, s // tk + 1):
        offs = jnp.where(lane == c, c_scr[:, c * tk - 1:c * tk].astype(I32), offs)
    offs_ref[...] = offs


def _route(logits_t, cap, tk):
    b, e, s = logits_t.shape
    body = functools.partial(_route_body, cap=cap, tk=tk)
    return pl.pallas_call(
        body,
        grid=(b,),
        in_specs=[pl.BlockSpec((None, e, s), lambda i: (i, 0, 0))],
        out_specs=[pl.BlockSpec((None, e, s), lambda i: (i, 0, 0)),
                   pl.BlockSpec((None, e, s), lambda i: (i, 0, 0)),
                   pl.BlockSpec((None, e, LANES), lambda i: (i, 0, 0))],
        out_shape=[SDS((b, e, s), I32), SDS((b, e, s), F32), SDS((b, e, LANES), I32)],
        scratch_shapes=[pltpu.VMEM((e, s), F32), pltpu.VMEM((e, s), F32)],
        compiler_params=_cparams(("parallel",)),
        name="route",
    )(logits_t)


def _window_plan(offs_ref, row0, ci, n_exp, row_stride, mp):
    starts, npieces = [], 0
    for k in range(n_exp):
        at = (row0 + k) * row_stride + ci
        off = offs_ref[at]
        n = offs_ref[at + 1] - off
        start = (off // SLOT_ALIGN) * SLOT_ALIGN
        starts.append(start)
        npieces = jnp.maximum(npieces, jnp.where(n > 0, (off - start + n + mp - 1) // mp, 0))
    return starts, npieces


def _gather_body(offs_ref, posm_ref, h_ref, o_ref, *, eg, mp, cap, row_stride):
    bi, gi, ci = pl.program_id(0), pl.program_id(1), pl.program_id(2)
    tk = h_ref.shape[0]

    @pl.when(ci == 0)
    def _():
        o_ref[...] = jnp.zeros_like(o_ref)

    starts, npieces = _window_plan(offs_ref, bi * N_EXPERTS + gi * eg, ci, eg, row_stride, mp)
    riota = lax.broadcasted_iota(I32, (mp, tk), 0)

    def piece(p, carry):
        bands = [jnp.where(riota == posm_ref[k] - (starts[k] + p * mp), 1.0, 0.0).astype(BF16) for k in range(eg)]
        res = jnp.dot(jnp.concatenate(bands, axis=0), h_ref[...], preferred_element_type=F32)
        for k in range(eg):
            dst = pl.multiple_of(jnp.minimum(starts[k] + p * mp, cap), SLOT_ALIGN)
            o_ref[k, pl.ds(dst, mp), :] = o_ref[k, pl.ds(dst, mp), :] + res[k * mp:(k + 1) * mp].astype(BF16)
        return carry

    lax.fori_loop(0, npieces, piece, 0)


def _gather(offs_flat, posm4, h3, cap, tk, eg, mp):
    b, e, _, s = posm4.shape
    nch = s // tk
    body = functools.partial(_gather_body, eg=eg, mp=mp, cap=cap, row_stride=LANES)
    grid_spec = pltpu.PrefetchScalarGridSpec(
        num_scalar_prefetch=1,
        grid=(b, e // eg, nch),
        in_specs=[pl.BlockSpec((None, eg, 1, tk), lambda bi, gi, ci, m: (bi, gi, 0, ci)),
                  pl.BlockSpec((None, tk, D_MODEL), lambda bi, gi, ci, m: (bi, ci, 0))],
        out_specs=pl.BlockSpec((None, eg, cap + mp, D_MODEL), lambda bi, gi, ci, m: (bi, gi, 0, 0)),
    )
    return pl.pallas_call(
        body,
        grid_spec=grid_spec,
        out_shape=SDS((b, e, cap + mp, D_MODEL), BF16),
        compiler_params=_cparams(("parallel", "parallel", "arbitrary")),
        name="moe_gather",
    )(offs_flat, posm4, h3)


def _ffn_body(x_ref, wg_ref, wu_ref, wd_ref, o_ref, acc):
    f = pl.program_id(2)
    x = x_ref[...]
    g = jnp.dot(x, wg_ref[...].astype(BF16), preferred_element_type=F32)
    u = jnp.dot(x, wu_ref[...].astype(BF16), preferred_element_type=F32)
    hid = (g * _sigmoid(g) * u).astype(BF16)
    y = jnp.dot(hid, wd_ref[...].astype(BF16), preferred_element_type=F32)

    @pl.when(f == 0)
    def _():
        acc[...] = y

    @pl.when(f > 0)
    def _():
        acc[...] += y

    @pl.when(f == pl.num_programs(2) - 1)
    def _():
        o_ref[...] = acc[...].astype(BF16)


def _expert_ffn(xg, w_gate, w_up, w_down, cap, tf):
    b, e = xg.shape[0], xg.shape[1]
    return pl.pallas_call(
        _ffn_body,
        grid=(b, e, EXPERT_FF // tf),
        in_specs=[pl.BlockSpec((None, None, cap, D_MODEL), lambda bi, ei, f: (bi, ei, 0, 0)),
                  pl.BlockSpec((None, D_MODEL, tf), lambda bi, ei, f: (ei, 0, f)),
                  pl.BlockSpec((None, D_MODEL, tf), lambda bi, ei, f: (ei, 0, f)),
                  pl.BlockSpec((None, tf, D_MODEL), lambda bi, ei, f: (ei, f, 0))],
        out_specs=pl.BlockSpec((None, None, cap, D_MODEL), lambda bi, ei, f: (bi, ei, 0, 0)),
        out_shape=SDS((b, e, cap, D_MODEL), BF16),
        scratch_shapes=[pltpu.VMEM((cap, D_MODEL), F32)],
        compiler_params=_cparams(("parallel", "parallel", "arbitrary")),
        name="expert_ffn",
    )(xg, w_gate, w_up, w_down)


def _combine_body(offs_ref, x_ref, posm_ref, gate_ref, y_hbm, o_ref, ybuf, sem, *, mp, cap, row_stride):
    bi, ci = pl.program_id(0), pl.program_id(1)
    nch = pl.num_programs(1)
    step = bi * nch + ci
    slot = step % 2
    tk = x_ref.shape[0]
    riota = lax.broadcasted_iota(I32, (mp, tk), 0)

    def window_copies(b, starts, p, to_slot):
        copies = []
        for k in range(N_EXPERTS):
            src = pl.multiple_of(jnp.minimum(starts[k] + p * mp, cap - mp), SLOT_ALIGN)
            copies.append(pltpu.make_async_copy(y_hbm.at[b, k, pl.ds(src, mp), :],
                                                ybuf.at[to_slot, pl.ds(k * mp, mp), :], sem.at[to_slot]))
        return copies

    starts, npieces = _window_plan(offs_ref, bi * N_EXPERTS, ci, N_EXPERTS, row_stride, mp)

    @pl.when(step == 0)
    def _():
        for cp in window_copies(bi, starts, 0, slot):
            cp.start()

    @pl.when(step + 1 < pl.num_programs(0) * nch)
    def _():
        nb = (step + 1) // nch
        nc = (step + 1) % nch
        nstarts, _ = _window_plan(offs_ref, nb * N_EXPERTS, nc, N_EXPERTS, row_stride, mp)
        for cp in window_copies(nb, nstarts, 0, 1 - slot):
            cp.start()

    def contribution(p):
        bands = []
        for k in range(N_EXPERTS):
            pos = posm_ref[k]
            lo = starts[k] + p * mp
            row = jnp.where((pos >= lo) & (pos < lo + mp), pos - jnp.minimum(lo, cap - mp), -1)
            bands.append(jnp.where(riota == row, gate_ref[k], 0.0).astype(BF16))
        w = jnp.concatenate(bands, axis=0)
        return lax.dot_general(w, ybuf[slot], TN_DIMS, preferred_element_type=F32)

    for cp in window_copies(bi, starts, 0, slot):
        cp.wait()
    o_ref[...] = x_ref[...] + contribution(0)

    def more(p, carry):
        copies = window_copies(bi, starts, p, slot)
        for cp in copies:
            cp.start()
        for cp in copies:
            cp.wait()
        o_ref[...] += contribution(p)
        return carry

    lax.fori_loop(1, npieces, more, 0)


def _combine(offs_flat, x3, posm4, gate4, y4, cap, tk, mp):
    b, s, _ = x3.shape
    body = functools.partial(_combine_body, mp=mp, cap=cap, row_stride=LANES)
    route_spec = pl.BlockSpec((None, N_EXPERTS, 1, tk), lambda bi, ci, m: (bi, 0, 0, ci))
    grid_spec = pltpu.PrefetchScalarGridSpec(
        num_scalar_prefetch=1,
        grid=(b, s // tk),
        in_specs=[pl.BlockSpec((None, tk, D_MODEL), lambda bi, ci, m: (bi, ci, 0)),
                  route_spec, route_spec,
                  pl.BlockSpec(memory_space=pl.ANY)],
        out_specs=pl.BlockSpec((None, tk, D_MODEL), lambda bi, ci, m: (bi, ci, 0)),
        scratch_shapes=[pltpu.VMEM((2, N_EXPERTS * mp, D_MODEL), BF16), pltpu.SemaphoreType.DMA((2,))],
    )
    return pl.pallas_call(
        body,
        grid_spec=grid_spec,
        out_shape=SDS((b, s, D_MODEL), F32),
        compiler_params=_cparams(("arbitrary", "arbitrary")),
        name="moe_combine",
    )(offs_flat, x3, posm4, gate4, y4)


def _repack_w_in(w_in):
    na_w = NA_HEADS * HEAD_DIM
    o = np.cumsum([0, na_w, na_w, na_w, 512, 128, 128, 512, 1024, 16, 1024])
    naq, nak, nav, waq, wak, wav, z, xbc, dt, conf = [w_in[..., o[i]:o[i + 1]] for i in range(10)]
    w_main = jnp.concatenate([xbc, conf, naq, nak, nav, waq, z, wak, wav], axis=-1).astype(BF16)
    w_dt = jnp.pad(dt, ((0, 0), (0, 0), (0, LANES - dt.shape[-1]))).astype(BF16)
    return w_main, w_dt


def _pad_lanes(v, width=LANES):
    return jnp.pad(v, [(0, 0)] * (v.ndim - 1) + [(0, width - v.shape[-1])])


def kernel(x, mix_norm_w, w_in, na_q_norm, na_k_norm, na_rpb, wa_q_norm, wa_k_norm, wa_sink, ssm_conv_w, ssm_conv_b, ssm_dt_bias, ssm_a_log, ssm_d, ssm_norm_w, conf_dw_w, conf_dw_b, conf_ln_w, conf_ln_b, w_out, ffn_norm_w, w_router, w_gate, w_up, w_down):
    b, s, d = x.shape
    depth = w_in.shape[0]
    rows = s // GRID_W
    cap = EC_CAPACITY * s // N_EXPERTS
    r_blk, w_blk = 4, 12
    tk = min(512, s)
    mp = min(128, cap)

    w_main, w_dt = _repack_w_in(w_in)
    w_out4 = w_out.astype(BF16).reshape(depth, 4, 512, D_MODEL)
    wr_pad = _pad_lanes(w_router)
    wr_hi = wr_pad.astype(BF16)
    wr_lo = (wr_pad - wr_hi.astype(F32)).astype(BF16)
    cos_t, sin_t = _rope_tables(s)
    bd = jnp.asarray(np.kron(np.eye(LANES // HEAD_DIM), np.full((HEAD_DIM, HEAD_DIM), 1.0 / HEAD_DIM)), BF16)
    eh_np = np.zeros((2, LANES, SSM_HEADS * HEAD_DIM), np.float32)
    for dr in range(2):
        for h in range(SSM_HEADS):
            eh_np[dr, dr * SSM_HEADS + h, h * HEAD_DIM:(h + 1) * HEAD_DIM] = 1.0
    eh = jnp.asarray(eh_np, BF16)
    tile2 = lambda v: jnp.tile(v, (1, LANES // HEAD_DIM))[:, None, :]
    naq_w, nak_w, waq_w, wak_w = tile2(na_q_norm), tile2(na_k_norm), tile2(wa_q_norm), tile2(wa_k_norm)
    dtb = _pad_lanes(ssm_dt_bias.reshape(depth, 1, 2 * SSM_HEADS))
    a_vec = _pad_lanes(-jnp.exp(ssm_a_log.reshape(depth, 1, 2 * SSM_HEADS)))
    dvec = jnp.repeat(ssm_d, HEAD_DIM, axis=-1)[:, None, :]
    conv_w = jnp.pad(ssm_conv_w, ((0, 0), (0, 8 - SSM_CONV), (0, 0)))
    conf_w = jnp.pad(conf_dw_w, ((0, 0), (0, 32 - CONF_KERNEL), (0, 0)))
    bias_tab = _na_bias_table(na_rpb, r_blk, w_blk, rows)

    for l in range(depth):
        u, dt_raw = _in_proj(x.reshape(b * s, d), mix_norm_w[l][None], w_main[l], w_dt[l])
        u3 = u.reshape(b, s, U_WIDTH)
        dt3 = dt_raw.reshape(b, s, LANES)
        o_a = _na_attention(u3, naq_w[l], nak_w[l], bias_tab[l], bd, r_blk, w_blk)
        o_b = _wa_attention(u3, wa_sink[l][None], cos_t, sin_t, waq_w[l], wak_w[l], bd, min(256, s))
        ssd_args = (u3, dt3, conv_w[l], ssm_conv_b[l][None], dtb[l], a_vec[l])
        y_f = _ssd_pass(*ssd_args, eh[0], None, reverse=False)
        o_c = _ssd_pass(*ssd_args, eh[1], (y_f, dvec[l], ssm_norm_w[l][None]), reverse=True)
        o_d = _conformer(u3, conf_w[l], conf_dw_b[l][None], conf_ln_w[l][None], conf_ln_b[l][None], min(256, s))
        x, h2, logits_t = _out_proj(o_a, o_b, o_c, o_d, w_out4[l], x, ffn_norm_w[l][None], wr_hi[l], wr_lo[l],
                                    min(512, s))
        posm, gate, offs = _route(logits_t, cap, tk)
        offs_flat = offs.reshape(-1)
        posm4 = posm.reshape(b, N_EXPERTS, 1, s)
        gate4 = gate.reshape(b, N_EXPERTS, 1, s)
        xg = _gather(offs_flat, posm4, h2, cap, tk, 4, mp)
        y4 = _expert_ffn(xg, w_gate[l], w_up[l], w_down[l], cap, 256)
        x = _combine(offs_flat, x, posm4, gate4, y4, cap, tk, mp)
    return x
```

```python
import functools

import numpy as np
import jax
import jax.numpy as jnp
from jax import lax
from jax.experimental import pallas as pl
from jax.experimental.pallas import tpu as pltpu

F32 = jnp.float32
BF16 = jnp.bfloat16
I32 = jnp.int32
SDS = jax.ShapeDtypeStruct

D_MODEL = 2048
HEAD_DIM = 64
EPS = 1e-6
NA_HEADS = 8
NA_KH = 8
NA_KW = 16
GRID_W = 64
WA_Q_HEADS = 8
WA_KV_HEADS = 2
WINDOW = 128
ROPE_THETA = 500000.0
ROPE_DIM = HEAD_DIM // 4
SSM_HEADS = 8
SSM_STATE = 128
SSM_CONV = 5
SSM_CHUNK = 128
CONF_WIDTH = 512
CONF_KERNEL = 31
N_EXPERTS = 16
EC_CAPACITY = 2
EXPERT_FF = D_MODEL // 2

U_XBC = 0
U_CONF = 1024
U_NAQ = 2048
U_NAK = 2560
U_NAV = 3072
U_WAQ = 3584
U_Z = 4096
U_WAK = 4608
U_WAV = 4736
U_WIDTH = 4864
LANES = 128
SUBLANES = 8
SLOT_ALIGN = 16
VMEM_LIMIT = 56 * 1024 * 1024

NT_DIMS = (((1,), (1,)), ((), ()))
TN_DIMS = (((0,), (0,)), ((), ()))
NEG = -1e30


def _cparams(sem):
    return pltpu.CompilerParams(dimension_semantics=sem, vmem_limit_bytes=VMEM_LIMIT)


def _split3(v):
    a = v.astype(BF16)
    r = v - a.astype(F32)
    b = r.astype(BF16)
    c = (r - b.astype(F32)).astype(BF16)
    return a, b, c


def _sigmoid(v):
    return 1.0 / (1.0 + jnp.exp(-v))


def _head_rms(xf, w, bd):
    sq = xf * xf
    hi = sq.astype(BF16)
    lo = (sq - hi.astype(F32)).astype(BF16)
    ms = (jnp.dot(hi, bd, preferred_element_type=F32)
          + jnp.dot(lo, bd, preferred_element_type=F32))
    return xf * lax.rsqrt(ms + EPS) * w


def _inproj_body(x_ref, nw_ref, w_ref, wdt_ref, u_ref, dt_ref, h_scr):
    @pl.when(pl.program_id(1) == 0)
    def _():
        x = x_ref[...]
        ms = jnp.mean(x * x, axis=-1, keepdims=True)
        h = (x * lax.rsqrt(ms + EPS) * nw_ref[...]).astype(BF16)
        h_scr[...] = h
        dt_ref[...] = jnp.dot(h, wdt_ref[...], preferred_element_type=F32)

    u_ref[...] = jnp.dot(h_scr[...], w_ref[...], preferred_element_type=F32).astype(BF16)


def _in_proj(x2, norm_w, w_main, w_dt, layer, tm=512):
    m = x2.shape[0]
    tn = U_WIDTH // 2
    return pl.pallas_call(
        _inproj_body,
        grid=(m // tm, U_WIDTH // tn),
        in_specs=[pl.BlockSpec((tm, D_MODEL), lambda i, j: (i, 0)),
                  pl.BlockSpec((1, D_MODEL), lambda i, j: (0, 0)),
                  pl.BlockSpec((None, D_MODEL, tn), lambda i, j: (layer, 0, j)),
                  pl.BlockSpec((None, D_MODEL, LANES), lambda i, j: (layer, 0, 0))],
        out_specs=[pl.BlockSpec((tm, tn), lambda i, j: (i, j)),
                   pl.BlockSpec((tm, LANES), lambda i, j: (i, 0))],
        out_shape=[SDS((m, U_WIDTH), BF16), SDS((m, LANES), F32)],
        scratch_shapes=[pltpu.VMEM((tm, D_MODEL), BF16)],
        compiler_params=_cparams(("parallel", "arbitrary")),
        name="in_proj",
    )(x2, norm_w, w_main, w_dt)


def _na_body(q_ref, k_ref, v_ref, qw_ref, kw_ref, bias_ref, bd_ref, o_ref, kn_scr, *, rq, wk, rows, r_blk, w_blk):
    s = q_ref.shape[0]
    nrb = rows // r_blk
    bd = bd_ref[...]
    ch = min(512, s)

    def kprep(i, c):
        s0 = pl.multiple_of(i * ch, ch)
        kf = k_ref[pl.ds(s0, ch), :].astype(F32)
        kn_scr[pl.ds(s0, ch), :] = _head_rms(kf, kw_ref[...], bd).astype(BF16)
        return c

    lax.fori_loop(0, s // ch, kprep, 0)
    lane = lax.broadcasted_iota(I32, (rq, LANES), 1)

    def blk(rb, c):
        q0 = pl.multiple_of(rb * rq, rq)
        w0 = pl.multiple_of(jnp.clip(rb * r_blk - NA_KH // 2, 0, rows - w_blk) * GRID_W, GRID_W)
        cls = jnp.where(rb == 0, 0, jnp.where(rb == nrb - 1, 2, 1))
        qn = _head_rms(q_ref[pl.ds(q0, rq), :].astype(F32), qw_ref[...], bd) * (HEAD_DIM ** -0.5)
        kw = kn_scr[pl.ds(w0, wk), :]
        vw = v_ref[pl.ds(w0, wk), :]
        outs = []
        for h in range(2):
            qh = jnp.where((lane // HEAD_DIM) == h, qn, 0.0).astype(BF16)
            lg = lax.dot_general(qh, kw, NT_DIMS, preferred_element_type=F32)
            lg = lg + bias_ref[h, cls].astype(F32)
            m = jnp.max(lg, axis=-1, keepdims=True)
            p = jnp.exp(lg - m)
            den = jnp.sum(p, axis=-1, keepdims=True)
            outs.append(jnp.dot(p.astype(BF16), vw, preferred_element_type=F32) * (1.0 / den))
        o_ref[pl.ds(q0, rq), :] = jnp.where(lane < HEAD_DIM, outs[0], outs[1]).astype(BF16)
        return c

    lax.fori_loop(0, nrb, blk, 0)


def _na_bias_table(rpb, r_blk, w_blk, rows):
    depth = rpb.shape[0]
    nrb = rows // r_blk
    col = np.arange(GRID_W)
    cstart = np.clip(col - NA_KW // 2, 0, GRID_W - NA_KW)
    valid_col = (col[None, :] >= cstart[:, None]) & (col[None, :] < cstart[:, None] + NA_KW)
    bj = np.clip(col[None, :] - col[:, None] + NA_KW - 1, 0, 2 * NA_KW - 2)
    sel_col = (bj[:, :, None] == np.arange(2 * NA_KW - 1)) & valid_col[:, :, None]
    sel_row, valid_rows = [], []
    for rb in (0, min(1, nrb - 1), nrb - 1):
        r = rb * r_blk + np.arange(r_blk)
        w0 = np.clip(rb * r_blk - NA_KH // 2, 0, rows - w_blk)
        kr = w0 + np.arange(w_blk)
        r0 = np.clip(r - NA_KH // 2, 0, rows - NA_KH)
        valid_row = (kr[None, :] >= r0[:, None]) & (kr[None, :] < r0[:, None] + NA_KH)
        bi = np.clip(kr[None, :] - r[:, None] + NA_KH - 1, 0, 2 * NA_KH - 2)
        sel_row.append((bi[:, :, None] == np.arange(2 * NA_KH - 1)) & valid_row[:, :, None])
        valid_rows.append(valid_row)
    sel_row = jnp.asarray(np.stack(sel_row), F32)
    mask = np.stack(valid_rows)[:, :, None, :, None] & valid_col[None, None, :, None, :]
    t = jnp.einsum('lhij,ckj->lhick', rpb, jnp.asarray(sel_col, F32), precision=lax.Precision.HIGHEST)
    t = jnp.einsum('lhick,grwi->lhgrcwk', t, sel_row, precision=lax.Precision.HIGHEST)
    t = jnp.where(jnp.asarray(mask)[None, None], t, NEG)
    return t.reshape(depth, NA_HEADS // 2, 2, 3, r_blk * GRID_W, w_blk * GRID_W).astype(BF16)


def _na_attention(u3, qw, kw, bias_tab, bd, r_blk, w_blk):
    b, s, _ = u3.shape
    rows = s // GRID_W
    rq, wk = r_blk * GRID_W, w_blk * GRID_W
    body = functools.partial(_na_body, rq=rq, wk=wk, rows=rows, r_blk=r_blk, w_blk=w_blk)
    return pl.pallas_call(
        body,
        grid=(b, NA_HEADS // 2),
        in_specs=[pl.BlockSpec((None, s, LANES), lambda i, h: (i, 0, U_NAQ // LANES + h)),
                  pl.BlockSpec((None, s, LANES), lambda i, h: (i, 0, U_NAK // LANES + h)),
                  pl.BlockSpec((None, s, LANES), lambda i, h: (i, 0, U_NAV // LANES + h)),
                  pl.BlockSpec((1, LANES), lambda i, h: (0, 0)),
                  pl.BlockSpec((1, LANES), lambda i, h: (0, 0)),
                  pl.BlockSpec((None, 2, 3, rq, wk), lambda i, h: (h, 0, 0, 0, 0)),
                  pl.BlockSpec((LANES, LANES), lambda i, h: (0, 0))],
        out_specs=pl.BlockSpec((None, s, LANES), lambda i, h: (i, 0, h)),
        out_shape=SDS((b, s, NA_HEADS * HEAD_DIM), BF16),
        scratch_shapes=[pltpu.VMEM((s, LANES), BF16)],
        compiler_params=_cparams(("parallel", "parallel")),
        name="na2d",
    )(u3, u3, u3, qw, kw, bias_tab, bd)


def _rope(xf, cos, sin):
    lane = lax.broadcasted_iota(I32, xf.shape, 1)
    half = ROPE_DIM // 2
    partner = jnp.where((lane % HEAD_DIM) < half, pltpu.roll(xf, LANES - half, 1), pltpu.roll(xf, half, 1))
    return xf * cos + partner * sin


def _wa_body(sink_ref, q_ref, k_ref, v_ref, cos_ref, sin_ref, qw_ref, kw_ref, bd_ref, o_ref, kd_scr, vd_scr,
             *, qb, wkb):
    s = k_ref.shape[0]
    kvh = pl.program_id(1)
    n = pl.program_id(2)
    bd = bd_ref[...]
    ch = min(512, s)

    @pl.when(n == 0)
    def _():
        lane = lax.broadcasted_iota(I32, (ch, LANES), 1)
        mine = (lane // HEAD_DIM) == kvh

        def prep(i, c):
            s0 = pl.multiple_of(i * ch, ch)
            kf = k_ref[pl.ds(s0, ch), :].astype(F32)
            kf = _rope(_head_rms(kf, kw_ref[...], bd), cos_ref[pl.ds(s0, ch), :], sin_ref[pl.ds(s0, ch), :])
            kd_scr[pl.ds(s0, ch), :] = jnp.where(mine, kf, pltpu.roll(kf, HEAD_DIM, 1)).astype(BF16)
            vf = v_ref[pl.ds(s0, ch), :].astype(F32)
            vd_scr[pl.ds(s0, ch), :] = jnp.where(mine, vf, pltpu.roll(vf, HEAD_DIM, 1)).astype(BF16)
            return c

        lax.fori_loop(0, s // ch, prep, 0)

    q0 = pl.multiple_of(n * qb, qb)
    ks = pl.multiple_of(jnp.clip(n * qb - WINDOW, 0, s - wkb), LANES)
    cosq = cos_ref[pl.ds(q0, qb), :]
    sinq = sin_ref[pl.ds(q0, qb), :]
    kw = kd_scr[pl.ds(ks, wkb), :]
    vw = vd_scr[pl.ds(ks, wkb), :]
    qpos = q0 + lax.broadcasted_iota(I32, (qb, wkb), 0)
    kpos = ks + lax.broadcasted_iota(I32, (qb, wkb), 1)
    allowed = jnp.abs(kpos - qpos) <= WINDOW
    lane = lax.broadcasted_iota(I32, (qb, LANES), 1)
    for half in range(2):
        qf = q_ref[:, half * LANES:(half + 1) * LANES].astype(F32)
        qn = _rope(_head_rms(qf, qw_ref[...], bd), cosq, sinq) * (HEAD_DIM ** -0.5)
        outs = []
        for gg in range(2):
            qh = jnp.where((lane // HEAD_DIM) == gg, qn, 0.0).astype(BF16)
            lg = lax.dot_general(qh, kw, NT_DIMS, preferred_element_type=F32)
            lg = jnp.where(allowed, lg, NEG)
            sk = sink_ref[0, kvh * 4 + half * 2 + gg]
            m = jnp.maximum(jnp.max(lg, axis=-1, keepdims=True), sk)
            p = jnp.exp(lg - m)
            den = jnp.sum(p, axis=-1, keepdims=True) + jnp.exp(sk - m)
            outs.append(jnp.dot(p.astype(BF16), vw, preferred_element_type=F32) * (1.0 / den))
        o_ref[:, half * LANES:(half + 1) * LANES] = jnp.where(lane < HEAD_DIM, outs[0], outs[1]).astype(BF16)


def _wa_attention(u3, sink, cos_t, sin_t, qw, kw, bd, qb):
    b, s, _ = u3.shape
    wkb = qb + 2 * WINDOW
    body = functools.partial(_wa_body, qb=qb, wkb=wkb)
    return pl.pallas_call(
        body,
        grid=(b, WA_KV_HEADS, s // qb),
        in_specs=[pl.BlockSpec(memory_space=pltpu.SMEM),
                  pl.BlockSpec((None, qb, 2 * LANES), lambda i, h, n: (i, n, U_WAQ // (2 * LANES) + h)),
                  pl.BlockSpec((None, s, LANES), lambda i, h, n: (i, 0, U_WAK // LANES)),
                  pl.BlockSpec((None, s, LANES), lambda i, h, n: (i, 0, U_WAV // LANES)),
                  pl.BlockSpec((s, LANES), lambda i, h, n: (0, 0)),
                  pl.BlockSpec((s, LANES), lambda i, h, n: (0, 0)),
                  pl.BlockSpec((1, LANES), lambda i, h, n: (0, 0)),
                  pl.BlockSpec((1, LANES), lambda i, h, n: (0, 0)),
                  pl.BlockSpec((LANES, LANES), lambda i, h, n: (0, 0))],
        out_specs=pl.BlockSpec((None, qb, 2 * LANES), lambda i, h, n: (i, n, h)),
        out_shape=SDS((b, s, WA_Q_HEADS * HEAD_DIM), BF16),
        scratch_shapes=[pltpu.VMEM((s, LANES), BF16), pltpu.VMEM((s, LANES), BF16)],
        compiler_params=_cparams(("parallel", "arbitrary", "arbitrary")),
        name="swa_gqa",
    )(sink, u3, u3, u3, cos_t, sin_t, qw, kw, bd)


def _rope_tables(s):
    half = ROPE_DIM // 2
    inv_freq = 1.0 / (ROPE_THETA ** (jnp.arange(half, dtype=F32) * 2.0 / ROPE_DIM))
    ang = jnp.arange(s, dtype=F32)[:, None] * inv_freq[None, :]
    cos, sin = jnp.cos(ang), jnp.sin(ang)
    ones = jnp.ones((s, HEAD_DIM - ROPE_DIM), F32)
    cos_h = jnp.concatenate([cos, cos, ones], axis=1)
    sin_h = jnp.concatenate([-sin, sin, 0.0 * ones], axis=1)
    return jnp.tile(cos_h, (1, LANES // HEAD_DIM)), jnp.tile(sin_h, (1, LANES // HEAD_DIM))


def _ssd_body(*refs, reverse, nc, final):
    if final:
        (xp_ref, xc_ref, xn_ref, dt_ref, cw_ref, cb_ref, dtb_ref, a_ref, eh_ref,
         yf_ref, z_ref, dv_ref, nw_ref, o_ref, st_scr, cv_scr) = refs
    else:
        (xp_ref, xc_ref, xn_ref, dt_ref, cw_ref, cb_ref, dtb_ref, a_ref, eh_ref,
         o_ref, st_scr, cv_scr) = refs
    ci = pl.program_id(1)
    c = (nc - 1 - ci) if reverse else ci
    L = SSM_CHUNK
    sw = SSM_HEADS * HEAD_DIM

    @pl.when(ci == 0)
    def _():
        st_scr[...] = jnp.zeros_like(st_scr)

    has_prev = jnp.where(c > 0, 1.0, 0.0)
    has_next = jnp.where(c < nc - 1, 1.0, 0.0)
    cv_scr[0:8, :] = xp_ref[L - 8:L, :].astype(F32) * has_prev
    cv_scr[8:8 + L, :] = xc_ref[...].astype(F32)
    cv_scr[8 + L:16 + L, :] = xn_ref[0:8, :].astype(F32) * has_next
    left = (SSM_CONV - 1) // 2
    acc = cb_ref[...] + cw_ref[0:1, :] * cv_scr[pl.ds(8 - left, L), :]
    for k in range(1, SSM_CONV):
        acc = acc + cw_ref[k:k + 1, :] * cv_scr[pl.ds(8 - left + k, L), :]
    xa = acc * _sigmoid(acc)
    xs = xa[:, 0:sw]

    raw = dt_ref[...] + dtb_ref[...]
    dtv = jnp.maximum(raw, 0.0) + jnp.log1p(jnp.exp(-jnp.abs(raw)))
    da = dtv * a_ref[...]
    ri = lax.broadcasted_iota(I32, (L, L), 0)
    cj = lax.broadcasted_iota(I32, (L, L), 1)
    keep = (cj >= ri) if reverse else (cj <= ri)
    tri = jnp.where(keep, 1.0, 0.0).astype(BF16)
    d1, d2, d3 = _split3(da)
    cum = (jnp.dot(tri, d1, preferred_element_type=F32) + jnp.dot(tri, d2, preferred_element_type=F32)
           + jnp.dot(tri, d3, preferred_element_type=F32))
    cum_t = cum.T
    eh = eh_ref[...]
    c1, c2, c3 = _split3(cum)
    cum_x = (jnp.dot(c1, eh, preferred_element_type=F32) + jnp.dot(c2, eh, preferred_element_type=F32)
             + jnp.dot(c3, eh, preferred_element_type=F32))
    t1, t2, t3 = _split3(dtv)
    dt_x = (jnp.dot(t1, eh, preferred_element_type=F32) + jnp.dot(t2, eh, preferred_element_type=F32)
            + jnp.dot(t3, eh, preferred_element_type=F32))
    tot_x = cum_x[0:1, :] if reverse else cum_x[L - 1:L, :]
    xc = xs * dt_x
    xdec = (xc * jnp.exp(tot_x - cum_x)).astype(BF16)
    xcb = xc.astype(BF16)
    ecum = jnp.exp(cum_x)
    etot = jnp.exp(tot_x)
    lane = lax.broadcasted_iota(I32, (L, LANES), 1)
    doff = SSM_HEADS if reverse else 0
    for g in range(2):
        bg = xa[:, sw + g * SSM_STATE: sw + (g + 1) * SSM_STATE]
        cg = xa[:, sw + 2 * SSM_STATE + g * SSM_STATE: sw + 2 * SSM_STATE + (g + 1) * SSM_STATE]
        cgb = cg.astype(BF16)
        cb = lax.dot_general(cgb, bg.astype(BF16), NT_DIMS, preferred_element_type=F32)
        bgt = bg.T.astype(BF16)
        for pp in range(2):
            p = g * 2 + pp
            sl = slice(p * LANES, (p + 1) * LANES)
            ys = []
            for hh in range(2):
                j = doff + 2 * p + hh
                seg = cum[:, j:j + 1] - cum_t[j:j + 1, :]
                dec = jnp.exp(jnp.where(keep, seg, NEG))
                ys.append(jnp.dot((cb * dec).astype(BF16), xcb[:, sl], preferred_element_type=F32))
            y = jnp.where(lane < HEAD_DIM, ys[0], ys[1])
            prev = st_scr[p]
            y = y + jnp.dot(cgb, prev.astype(BF16), preferred_element_type=F32) * ecum[:, sl]
            st_scr[p] = etot[:, sl] * prev + jnp.dot(bgt, xdec[:, sl], preferred_element_type=F32)
            if final:
                y = y + yf_ref[:, sl] + xs[:, sl] * dv_ref[:, sl]
                zf = z_ref[:, sl].astype(F32)
                cv_scr[0:L, sl] = y * (zf * _sigmoid(zf))
            else:
                o_ref[:, sl] = y
    if final:
        gated = cv_scr[0:L, 0:sw]
        ms = jnp.mean(gated * gated, axis=-1, keepdims=True)
        o_ref[...] = (gated * lax.rsqrt(ms + EPS) * nw_ref[...]).astype(BF16)


def _ssd_pass(u3, dt3, cw, cb, dtb, a_vec, eh, extra, reverse):
    b, s, _ = u3.shape
    L = SSM_CHUNK
    nc = s // L
    sw = SSM_HEADS * HEAD_DIM
    final = extra is not None

    def cidx(ci):
        return (nc - 1 - ci) if reverse else ci

    xw = 2 * sw
    in_specs = [pl.BlockSpec((None, L, xw), lambda i, ci: (i, jnp.maximum(cidx(ci) - 1, 0), U_XBC // xw)),
                pl.BlockSpec((None, L, xw), lambda i, ci: (i, cidx(ci), U_XBC // xw)),
                pl.BlockSpec((None, L, xw), lambda i, ci: (i, jnp.minimum(cidx(ci) + 1, nc - 1), U_XBC // xw)),
                pl.BlockSpec((None, L, LANES), lambda i, ci: (i, cidx(ci), 0)),
                pl.BlockSpec((8, xw), lambda i, ci: (0, 0)),
                pl.BlockSpec((1, xw), lambda i, ci: (0, 0)),
                pl.BlockSpec((1, LANES), lambda i, ci: (0, 0)),
                pl.BlockSpec((1, LANES), lambda i, ci: (0, 0)),
                pl.BlockSpec((LANES, sw), lambda i, ci: (0, 0))]
    args = [u3, u3, u3, dt3, cw, cb, dtb, a_vec, eh]
    if final:
        y_f, dvec, nw = extra
        in_specs += [pl.BlockSpec((None, L, sw), lambda i, ci: (i, cidx(ci), 0)),
                     pl.BlockSpec((None, L, sw), lambda i, ci: (i, cidx(ci), U_Z // sw)),
                     pl.BlockSpec((1, sw), lambda i, ci: (0, 0)),
                     pl.BlockSpec((1, sw), lambda i, ci: (0, 0))]
        args += [y_f, u3, dvec, nw]
    body = functools.partial(_ssd_body, reverse=reverse, nc=nc, final=final)
    return pl.pallas_call(
        body,
        grid=(b, nc),
        in_specs=in_specs,
        out_specs=pl.BlockSpec((None, L, sw), lambda i, ci: (i, cidx(ci), 0)),
        out_shape=SDS((b, s, sw), BF16 if final else F32),
        scratch_shapes=[pltpu.VMEM((4, SSM_STATE, LANES), F32), pltpu.VMEM((L + 16, xw), F32)],
        compiler_params=_cparams(("parallel", "arbitrary")),
        name="ssd_bwd" if reverse else "ssd_fwd",
    )(*args)


def _conf_body(xp_ref, xc_ref, xn_ref, w_ref, b_ref, lnw_ref, lnb_ref, o_ref, scr, ph_scr, *, tc, nt):
    i = pl.program_id(1)
    cwid = CONF_WIDTH
    halo = 16

    def glu(v):
        return v[:, 0:cwid].astype(F32) * _sigmoid(v[:, cwid:2 * cwid].astype(F32))

    scr[0:halo, :] = glu(xp_ref[tc - halo:tc, :]) * jnp.where(i > 0, 1.0, 0.0)
    scr[halo:halo + tc, :] = glu(xc_ref[...])
    scr[halo + tc:2 * halo + tc, :] = glu(xn_ref[0:halo, :]) * jnp.where(i < nt - 1, 1.0, 0.0)
    first = halo - (CONF_KERNEL - 1) // 2
    acc = b_ref[...]
    for ph in range(SUBLANES):
        part = None
        for row in range(ph, first + CONF_KERNEL, SUBLANES):
            k = row - first
            if 0 <= k < CONF_KERNEL:
                term = w_ref[k:k + 1, :] * scr[pl.ds(row - ph, tc + SUBLANES), :]
                part = term if part is None else part + term
        if ph == 0:
            acc = acc + part[0:tc]
        else:
            ph_scr[...] = part
            acc = acc + ph_scr[pl.ds(ph, tc), :]
    mu = jnp.mean(acc, axis=-1, keepdims=True)
    cen = acc - mu
    var = jnp.mean(cen * cen, axis=-1, keepdims=True)
    y = cen * lax.rsqrt(var + EPS) * lnw_ref[...] + lnb_ref[...]
    o_ref[...] = (y * _sigmoid(y)).astype(BF16)


def _conformer(u3, w, bvec, lnw, lnb, tc):
    b, s, _ = u3.shape
    nt = s // tc
    cw2 = 2 * CONF_WIDTH
    body = functools.partial(_conf_body, tc=tc, nt=nt)
    return pl.pallas_call(
        body,
        grid=(b, nt),
        in_specs=[pl.BlockSpec((None, tc, cw2), lambda i, t: (i, jnp.maximum(t - 1, 0), U_CONF // cw2)),
                  pl.BlockSpec((None, tc, cw2), lambda i, t: (i, t, U_CONF // cw2)),
                  pl.BlockSpec((None, tc, cw2), lambda i, t: (i, jnp.minimum(t + 1, nt - 1), U_CONF // cw2)),
                  pl.BlockSpec((32, CONF_WIDTH), lambda i, t: (0, 0)),
                  pl.BlockSpec((1, CONF_WIDTH), lambda i, t: (0, 0)),
                  pl.BlockSpec((1, CONF_WIDTH), lambda i, t: (0, 0)),
                  pl.BlockSpec((1, CONF_WIDTH), lambda i, t: (0, 0))],
        out_specs=pl.BlockSpec((None, tc, CONF_WIDTH), lambda i, t: (i, t, 0)),
        out_shape=SDS((b, s, CONF_WIDTH), BF16),
        scratch_shapes=[pltpu.VMEM((tc + 32, CONF_WIDTH), F32), pltpu.VMEM((tc + SUBLANES, CONF_WIDTH), F32)],
        compiler_params=_cparams(("parallel", "parallel")),
        name="conformer",
    )(u3, u3, u3, w, bvec, lnw, lnb)


def _outproj_body(oa_ref, ob_ref, oc_ref, od_ref, w_ref, x_ref, fw_ref, wr_ref, xo_ref, h_ref, lg_ref):
    mixed = jnp.concatenate([oa_ref[...], ob_ref[...], oc_ref[...], od_ref[...]], axis=1)
    acc = x_ref[...] + jnp.dot(mixed, w_ref[...], preferred_element_type=F32)
    xo_ref[...] = acc
    ms = jnp.mean(acc * acc, axis=-1, keepdims=True)
    hf = acc * lax.rsqrt(ms + EPS) * fw_ref[...]
    hb = hf.astype(BF16)
    h_ref[...] = hb
    lo = (hf - hb.astype(F32)).astype(BF16)
    wr = wr_ref[...]
    both = jnp.dot(hb, wr, preferred_element_type=F32)
    lg = both[:, 0:LANES] + both[:, LANES:2 * LANES] + jnp.dot(lo, wr[:, 0:LANES], preferred_element_type=F32)
    lg_ref[...] = lg.T[0:N_EXPERTS, :]


def _out_proj(o_a, o_b, o_c, o_d, w_out, layer, x3, fw, wr, tm):
    b, s, _ = x3.shape
    gw = 512
    mix_spec = pl.BlockSpec((None, tm, gw), lambda i, t: (i, t, 0))
    return pl.pallas_call(
        _outproj_body,
        grid=(b, s // tm),
        in_specs=[mix_spec, mix_spec, mix_spec, mix_spec,
                  pl.BlockSpec((None, D_MODEL, D_MODEL), lambda i, t: (layer, 0, 0)),
                  pl.BlockSpec((None, tm, D_MODEL), lambda i, t: (i, t, 0)),
                  pl.BlockSpec((1, D_MODEL), lambda i, t: (0, 0)),
                  pl.BlockSpec((D_MODEL, 2 * LANES), lambda i, t: (0, 0))],
        out_specs=[pl.BlockSpec((None, tm, D_MODEL), lambda i, t: (i, t, 0)),
                   pl.BlockSpec((None, tm, D_MODEL), lambda i, t: (i, t, 0)),
                   pl.BlockSpec((None, N_EXPERTS, tm), lambda i, t: (i, 0, t))],
        out_shape=[SDS((b, s, D_MODEL), F32), SDS((b, s, D_MODEL), BF16), SDS((b, N_EXPERTS, s), F32)],
        compiler_params=_cparams(("parallel", "parallel")),
        name="out_proj",
    )(o_a, o_b, o_c, o_d, w_out, x3, fw, wr)


def _route_body(lg_ref, posm_ref, gate_ref, offs_ref, m_scr, c_scr, *, cap, tk):
    e, s = lg_ref.shape
    cb = 256
    nblk = s // cb
    lg = lg_ref[...]
    mx = jnp.max(lg, axis=0, keepdims=True)
    ex = jnp.exp(lg - mx)
    aff = ex / jnp.sum(ex, axis=0, keepdims=True)
    gate_ref[...] = aff
    bits = pltpu.bitcast(aff, I32)

    def search(i, v):
        cand = v | jnp.left_shift(jnp.int32(1), 30 - i)
        cnt = jnp.sum(jnp.where(bits >= cand, 1.0, 0.0), axis=1, keepdims=True)
        return jnp.where(cnt >= cap, cand, v)

    thr = lax.fori_loop(0, 31, search, jnp.zeros((e, 1), I32))
    gt = bits > thr
    eq = bits == thr
    need = cap - jnp.sum(jnp.where(gt, 1.0, 0.0), axis=1, keepdims=True)
    ri = lax.broadcasted_iota(I32, (cb, cb), 0)
    cj = lax.broadcasted_iota(I32, (cb, cb), 1)
    tri = jnp.where(ri <= cj, 1.0, 0.0).astype(BF16)

    def cumsum_into_c():
        def blk(i, carry):
            o = pl.multiple_of(i * cb, cb)
            inc = jnp.dot(m_scr[:, pl.ds(o, cb)].astype(BF16), tri, preferred_element_type=F32) + carry
            c_scr[:, pl.ds(o, cb)] = inc
            return inc[:, cb - 1:cb]
        lax.fori_loop(0, nblk, blk, jnp.zeros((e, 1), F32))

    eqf = jnp.where(eq, 1.0, 0.0)
    m_scr[...] = eqf
    cumsum_into_c()
    sel = gt | (eq & ((c_scr[...] - eqf) < need))
    self_f = jnp.where(sel, 1.0, 0.0)
    m_scr[...] = self_f
    cumsum_into_c()
    posm_ref[...] = jnp.where(sel, (c_scr[...] - self_f).astype(I32), -1)
    lane = lax.broadcasted_iota(I32, (e, LANES), 1)
    offs = jnp.zeros((e, LANES), I32)
    for c in range(1, s // tk + 1):
        offs = jnp.where(lane == c, c_scr[:, c * tk - 1:c * tk].astype(I32), offs)
    offs_ref[...] = offs


def _route(logits_t, cap, tk):
    b, e, s = logits_t.shape
    body = functools.partial(_route_body, cap=cap, tk=tk)
    return pl.pallas_call(
        body,
        grid=(b,),
        in_specs=[pl.BlockSpec((None, e, s), lambda i: (i, 0, 0))],
        out_specs=[pl.BlockSpec((None, e, s), lambda i: (i, 0, 0)),
                   pl.BlockSpec((None, e, s), lambda i: (i, 0, 0)),
                   pl.BlockSpec((None, e, LANES), lambda i: (i, 0, 0))],
        out_shape=[SDS((b, e, s), I32), SDS((b, e, s), F32), SDS((b, e, LANES), I32)],
        scratch_shapes=[pltpu.VMEM((e, s), F32), pltpu.VMEM((e, s), F32)],
        compiler_params=_cparams(("parallel",)),
        name="route",
    )(logits_t)


def _window_plan(offs_ref, row0, ci, n_exp, row_stride, mp):
    starts, npieces = [], 0
    for k in range(n_exp):
        at = (row0 + k) * row_stride + ci
        off = offs_ref[at]
        n = offs_ref[at + 1] - off
        start = (off // SLOT_ALIGN) * SLOT_ALIGN
        starts.append(start)
        npieces = jnp.maximum(npieces, jnp.where(n > 0, (off - start + n + mp - 1) // mp, 0))
    return starts, npieces


def _gather_body(offs_ref, posm_ref, h_ref, o_ref, *, eg, mp, cap, row_stride):
    bi, gi, ci = pl.program_id(0), pl.program_id(1), pl.program_id(2)
    tk = h_ref.shape[0]

    @pl.when(ci == 0)
    def _():
        o_ref[...] = jnp.zeros_like(o_ref)

    starts, npieces = _window_plan(offs_ref, bi * N_EXPERTS + gi * eg, ci, eg, row_stride, mp)
    riota = lax.broadcasted_iota(I32, (mp, tk), 0)

    def piece(p, carry):
        bands = [jnp.where(riota == posm_ref[k] - (starts[k] + p * mp), 1.0, 0.0).astype(BF16) for k in range(eg)]
        res = jnp.dot(jnp.concatenate(bands, axis=0), h_ref[...], preferred_element_type=F32)
        for k in range(eg):
            dst = pl.multiple_of(jnp.minimum(starts[k] + p * mp, cap), SLOT_ALIGN)
            o_ref[k, pl.ds(dst, mp), :] = o_ref[k, pl.ds(dst, mp), :] + res[k * mp:(k + 1) * mp].astype(BF16)
        return carry

    lax.fori_loop(0, npieces, piece, 0)


def _gather(offs_flat, posm4, h3, cap, tk, eg, mp):
    b, e, _, s = posm4.shape
    nch = s // tk
    body = functools.partial(_gather_body, eg=eg, mp=mp, cap=cap, row_stride=LANES)
    grid_spec = pltpu.PrefetchScalarGridSpec(
        num_scalar_prefetch=1,
        grid=(b, e // eg, nch),
        in_specs=[pl.BlockSpec((None, eg, 1, tk), lambda bi, gi, ci, m: (bi, gi, 0, ci)),
                  pl.BlockSpec((None, tk, D_MODEL), lambda bi, gi, ci, m: (bi, ci, 0))],
        out_specs=pl.BlockSpec((None, eg, cap + mp, D_MODEL), lambda bi, gi, ci, m: (bi, gi, 0, 0)),
    )
    return pl.pallas_call(
        body,
        grid_spec=grid_spec,
        out_shape=SDS((b, e, cap + mp, D_MODEL), BF16),
        compiler_params=_cparams(("parallel", "parallel", "arbitrary")),
        name="moe_gather",
    )(offs_flat, posm4, h3)


def _ffn_body(x_ref, wg_ref, wu_ref, wd_ref, o_ref, acc):
    f = pl.program_id(2)
    x = x_ref[...]
    g = jnp.dot(x, wg_ref[...].astype(BF16), preferred_element_type=F32)
    u = jnp.dot(x, wu_ref[...].astype(BF16), preferred_element_type=F32)
    hid = (g * _sigmoid(g) * u).astype(BF16)

    @pl.when(f == 0)
    def _():
        acc[...] = jnp.zeros_like(acc)

    acc[...] += jnp.dot(hid, wd_ref[...].astype(BF16), preferred_element_type=F32)

    @pl.when(f == pl.num_programs(2) - 1)
    def _():
        o_ref[...] = acc[...].astype(BF16)


def _expert_ffn(xg, w_gate, w_up, w_down, layer, cap, tf):
    b, e = xg.shape[0], xg.shape[1]
    return pl.pallas_call(
        _ffn_body,
        grid=(b, e, EXPERT_FF // tf),
        in_specs=[pl.BlockSpec((None, None, cap, D_MODEL), lambda bi, ei, f: (bi, ei, 0, 0)),
                  pl.BlockSpec((None, None, D_MODEL, tf), lambda bi, ei, f: (layer, ei, 0, f)),
                  pl.BlockSpec((None, None, D_MODEL, tf), lambda bi, ei, f: (layer, ei, 0, f)),
                  pl.BlockSpec((None, None, tf, D_MODEL), lambda bi, ei, f: (layer, ei, f, 0))],
        out_specs=pl.BlockSpec((None, None, cap, D_MODEL), lambda bi, ei, f: (bi, ei, 0, 0)),
        out_shape=SDS((b, e, cap, D_MODEL), BF16),
        scratch_shapes=[pltpu.VMEM((cap, D_MODEL), F32)],
        compiler_params=_cparams(("parallel", "parallel", "arbitrary")),
        name="expert_ffn",
    )(xg, w_gate, w_up, w_down)


def _combine_body(offs_ref, x_ref, posm_ref, gate_ref, y_hbm, o_ref, ybuf, sem, *, mp, cap, row_stride):
    bi, ci = pl.program_id(0), pl.program_id(1)
    nch = pl.num_programs(1)
    step = bi * nch + ci
    slot = step % 2
    tk = x_ref.shape[0]
    riota = lax.broadcasted_iota(I32, (mp, tk), 0)

    def window_copies(b, starts, p, to_slot):
        copies = []
        for k in range(N_EXPERTS):
            src = pl.multiple_of(jnp.minimum(starts[k] + p * mp, cap - mp), SLOT_ALIGN)
            copies.append(pltpu.make_async_copy(y_hbm.at[b, k, pl.ds(src, mp), :],
                                                ybuf.at[to_slot, pl.ds(k * mp, mp), :], sem.at[to_slot]))
        return copies

    starts, npieces = _window_plan(offs_ref, bi * N_EXPERTS, ci, N_EXPERTS, row_stride, mp)

    @pl.when(step == 0)
    def _():
        for cp in window_copies(bi, starts, 0, slot):
            cp.start()

    @pl.when(step + 1 < pl.num_programs(0) * nch)
    def _():
        nb = (step + 1) // nch
        nc = (step + 1) % nch
        nstarts, _ = _window_plan(offs_ref, nb * N_EXPERTS, nc, N_EXPERTS, row_stride, mp)
        for cp in window_copies(nb, nstarts, 0, 1 - slot):
            cp.start()

    def contribution(p):
        bands = []
        for k in range(N_EXPERTS):
            pos = posm_ref[k]
            lo = starts[k] + p * mp
            row = jnp.where((pos >= lo) & (pos < lo + mp), pos - jnp.minimum(lo, cap - mp), -1)
            bands.append(jnp.where(riota == row, gate_ref[k], 0.0).astype(BF16))
        w = jnp.concatenate(bands, axis=0)
        return lax.dot_general(w, ybuf[slot], TN_DIMS, preferred_element_type=F32)

    for cp in window_copies(bi, starts, 0, slot):
        cp.wait()
    o_ref[...] = x_ref[...] + contribution(0)

    def more(p, carry):
        copies = window_copies(bi, starts, p, slot)
        for cp in copies:
            cp.start()
        for cp in copies:
            cp.wait()
        o_ref[...] += contribution(p)
        return carry

    lax.fori_loop(1, npieces, more, 0)


def _combine(offs_flat, x3, posm4, gate4, y4, cap, tk, mp):
    b, s, _ = x3.shape
    body = functools.partial(_combine_body, mp=mp, cap=cap, row_stride=LANES)
    route_spec = pl.BlockSpec((None, N_EXPERTS, 1, tk), lambda bi, ci, m: (bi, 0, 0, ci))
    grid_spec = pltpu.PrefetchScalarGridSpec(
        num_scalar_prefetch=1,
        grid=(b, s // tk),
        in_specs=[pl.BlockSpec((None, tk, D_MODEL), lambda bi, ci, m: (bi, ci, 0)),
                  route_spec, route_spec,
                  pl.BlockSpec(memory_space=pl.ANY)],
        out_specs=pl.BlockSpec((None, tk, D_MODEL), lambda bi, ci, m: (bi, ci, 0)),
        scratch_shapes=[pltpu.VMEM((2, N_EXPERTS * mp, D_MODEL), BF16), pltpu.SemaphoreType.DMA((2,))],
    )
    return pl.pallas_call(
        body,
        grid_spec=grid_spec,
        out_shape=SDS((b, s, D_MODEL), F32),
        compiler_params=_cparams(("arbitrary", "arbitrary")),
        name="moe_combine",
    )(offs_flat, x3, posm4, gate4, y4)


def _repack_w_in(w_in):
    na_w = NA_HEADS * HEAD_DIM
    o = np.cumsum([0, na_w, na_w, na_w, 512, 128, 128, 512, 1024, 16, 1024])
    naq, nak, nav, waq, wak, wav, z, xbc, dt, conf = [w_in[..., o[i]:o[i + 1]] for i in range(10)]
    w_main = jnp.concatenate([xbc, conf, naq, nak, nav, waq, z, wak, wav], axis=-1).astype(BF16)
    w_dt = jnp.pad(dt, ((0, 0), (0, 0), (0, LANES - dt.shape[-1]))).astype(BF16)
    return w_main, w_dt


def _pad_lanes(v, width=LANES):
    return jnp.pad(v, [(0, 0)] * (v.ndim - 1) + [(0, width - v.shape[-1])])


def kernel(x, mix_norm_w, w_in, na_q_norm, na_k_norm, na_rpb, wa_q_norm, wa_k_norm, wa_sink, ssm_conv_w, ssm_conv_b, ssm_dt_bias, ssm_a_log, ssm_d, ssm_norm_w, conf_dw_w, conf_dw_b, conf_ln_w, conf_ln_b, w_out, ffn_norm_w, w_router, w_gate, w_up, w_down):
    b, s, d = x.shape
    depth = w_in.shape[0]
    rows = s // GRID_W
    cap = EC_CAPACITY * s // N_EXPERTS
    r_blk, w_blk = 4, 12
    tk = min(512, s)
    mp = min(128, cap)

    w_main, w_dt = _repack_w_in(w_in)
    w_out_b = w_out.astype(BF16)
    wr_pad = _pad_lanes(w_router)
    wr_hi = wr_pad.astype(BF16)
    wr_cat = jnp.concatenate([wr_hi, (wr_pad - wr_hi.astype(F32)).astype(BF16)], axis=-1)
    cos_t, sin_t = _rope_tables(s)
    bd = jnp.asarray(np.kron(np.eye(LANES // HEAD_DIM), np.full((HEAD_DIM, HEAD_DIM), 1.0 / HEAD_DIM)), BF16)
    eh_np = np.zeros((2, LANES, SSM_HEADS * HEAD_DIM), np.float32)
    for dr in range(2):
        for h in range(SSM_HEADS):
            eh_np[dr, dr * SSM_HEADS + h, h * HEAD_DIM:(h + 1) * HEAD_DIM] = 1.0
    eh = jnp.asarray(eh_np, BF16)
    tile2 = lambda v: jnp.tile(v, (1, LANES // HEAD_DIM))[:, None, :]
    naq_w, nak_w, waq_w, wak_w = tile2(na_q_norm), tile2(na_k_norm), tile2(wa_q_norm), tile2(wa_k_norm)
    dtb = _pad_lanes(ssm_dt_bias.reshape(depth, 1, 2 * SSM_HEADS))
    a_vec = _pad_lanes(-jnp.exp(ssm_a_log.reshape(depth, 1, 2 * SSM_HEADS)))
    dvec = jnp.repeat(ssm_d, HEAD_DIM, axis=-1)[:, None, :]
    conv_w = jnp.pad(ssm_conv_w, ((0, 0), (0, 8 - SSM_CONV), (0, 0)))
    conf_w = jnp.pad(conf_dw_w, ((0, 0), (0, 32 - CONF_KERNEL), (0, 0)))
    bias_tab = _na_bias_table(na_rpb, r_blk, w_blk, rows)

    for l in range(depth):
        u, dt_raw = _in_proj(x.reshape(b * s, d), mix_norm_w[l][None], w_main, w_dt, l)
        u3 = u.reshape(b, s, U_WIDTH)
        dt3 = dt_raw.reshape(b, s, LANES)
        o_a = _na_attention(u3, naq_w[l], nak_w[l], bias_tab[l], bd, r_blk, w_blk)
        o_b = _wa_attention(u3, wa_sink[l][None], cos_t, sin_t, waq_w[l], wak_w[l], bd, min(256, s))
        ssd_args = (u3, dt3, conv_w[l], ssm_conv_b[l][None], dtb[l], a_vec[l])
        y_f = _ssd_pass(*ssd_args, eh[0], None, reverse=False)
        o_c = _ssd_pass(*ssd_args, eh[1], (y_f, dvec[l], ssm_norm_w[l][None]), reverse=True)
        o_d = _conformer(u3, conf_w[l], conf_dw_b[l][None], conf_ln_w[l][None], conf_ln_b[l][None], min(256, s))
        x, h2, logits_t = _out_proj(o_a, o_b, o_c, o_d, w_out_b, l, x, ffn_norm_w[l][None], wr_cat[l], min(512, s))
        posm, gate, offs = _route(logits_t, cap, tk)
        offs_flat = offs.reshape(-1)
        posm4 = posm.reshape(b, N_EXPERTS, 1, s)
        gate4 = gate.reshape(b, N_EXPERTS, 1, s)
        xg = _gather(offs_flat, posm4, h2, cap, tk, 4, mp)
        y4 = _expert_ffn(xg, w_gate, w_up, w_down, l, cap, 256)
        x = _combine(offs_flat, x, posm4, gate4, y4, cap, tk, mp)
    return x
```

```python
import functools

import numpy as np
import jax
import jax.numpy as jnp
from jax import lax
from jax.experimental import pallas as pl
from jax.experimental.pallas import tpu as pltpu

F32 = jnp.float32
BF16 = jnp.bfloat16
I32 = jnp.int32
SDS = jax.ShapeDtypeStruct

D_MODEL = 2048
HEAD_DIM = 64
EPS = 1e-6
NA_HEADS = 8
NA_KH = 8
NA_KW = 16
GRID_W = 64
WA_Q_HEADS = 8
WA_KV_HEADS = 2
WINDOW = 128
ROPE_THETA = 500000.0
ROPE_DIM = HEAD_DIM // 4
SSM_HEADS = 8
SSM_STATE = 128
SSM_CONV = 5
SSM_CHUNK = 128
CONF_WIDTH = 512
CONF_KERNEL = 31
N_EXPERTS = 16
EC_CAPACITY = 2
EXPERT_FF = D_MODEL // 2

U_XBC = 0
U_CONF = 1024
U_NAQ = 2048
U_NAK = 2560
U_NAV = 3072
U_WAQ = 3584
U_Z = 4096
U_WAK = 4608
U_WAV = 4736
U_WIDTH = 4864
LANES = 128
SUBLANES = 8
SLOT_ALIGN = 16
VMEM_LIMIT = 56 * 1024 * 1024

NT_DIMS = (((1,), (1,)), ((), ()))
TN_DIMS = (((0,), (0,)), ((), ()))
NEG = -1e30


def _cparams(sem):
    return pltpu.CompilerParams(dimension_semantics=sem, vmem_limit_bytes=VMEM_LIMIT)


def _split3(v):
    a = v.astype(BF16)
    r = v - a.astype(F32)
    b = r.astype(BF16)
    c = (r - b.astype(F32)).astype(BF16)
    return a, b, c


def _sigmoid(v):
    return 1.0 / (1.0 + jnp.exp(-v))


def _head_rms(xf, w, bd):
    sq = xf * xf
    hi = sq.astype(BF16)
    lo = (sq - hi.astype(F32)).astype(BF16)
    ms = (jnp.dot(hi, bd, preferred_element_type=F32)
          + jnp.dot(lo, bd, preferred_element_type=F32))
    return xf * lax.rsqrt(ms + EPS) * w


def _inproj_body(x_ref, nw_ref, w_ref, wdt_ref, u_ref, dt_ref, h_scr):
    @pl.when(pl.program_id(1) == 0)
    def _():
        x = x_ref[...]
        ms = jnp.mean(x * x, axis=-1, keepdims=True)
        h = (x * lax.rsqrt(ms + EPS) * nw_ref[...]).astype(BF16)
        h_scr[...] = h
        dt_ref[...] = jnp.dot(h, wdt_ref[...], preferred_element_type=F32)

    u_ref[...] = jnp.dot(h_scr[...], w_ref[...], preferred_element_type=F32).astype(BF16)


def _in_proj(x2, norm_w, w_main, w_dt, layer, tm=512):
    m = x2.shape[0]
    tn = U_WIDTH // 2
    return pl.pallas_call(
        _inproj_body,
        grid=(m // tm, U_WIDTH // tn),
        in_specs=[pl.BlockSpec((tm, D_MODEL), lambda i, j: (i, 0)),
                  pl.BlockSpec((1, D_MODEL), lambda i, j: (0, 0)),
                  pl.BlockSpec((None, D_MODEL, tn), lambda i, j: (layer, 0, j)),
                  pl.BlockSpec((None, D_MODEL, LANES), lambda i, j: (layer, 0, 0))],
        out_specs=[pl.BlockSpec((tm, tn), lambda i, j: (i, j)),
                   pl.BlockSpec((tm, LANES), lambda i, j: (i, 0))],
        out_shape=[SDS((m, U_WIDTH), BF16), SDS((m, LANES), F32)],
        scratch_shapes=[pltpu.VMEM((tm, D_MODEL), BF16)],
        compiler_params=_cparams(("parallel", "arbitrary")),
        name="in_proj",
    )(x2, norm_w, w_main, w_dt)


def _na_row_plan(r_blk, w_blk, rows):
    nrb = rows // r_blk
    plan = []
    for rb in (0, min(1, nrb - 1), nrb - 1):
        w0 = int(np.clip(rb * r_blk - NA_KH // 2, 0, rows - w_blk))
        cls = []
        for rr in range(r_blk):
            r = rb * r_blk + rr
            r0 = int(np.clip(r - NA_KH // 2, 0, rows - NA_KH))
            cls.append(tuple((r0 <= w0 + wi < r0 + NA_KH, w0 + wi - r + NA_KH - 1) for wi in range(w_blk)))
        plan.append(tuple(cls))
    return tuple(plan)


def _na_body(q_ref, k_ref, v_ref, qw_ref, kw_ref, cb_ref, bd_ref, o_ref, kn_scr, bias_ref,
             *, rq, wk, rows, r_blk, w_blk, row_plan):
    s = q_ref.shape[0]
    nrb = rows // r_blk
    bd = bd_ref[...]
    ch = min(512, s)

    lane_t = lax.broadcasted_iota(I32, (GRID_W, LANES), 1)
    masked = jnp.full((GRID_W, LANES), NEG, BF16)
    for h in range(2):
        for cls in range(3):
            for rr in range(r_blk):
                for m2 in range(w_blk // 2):
                    (ok0, i0), (ok1, i1) = row_plan[cls][rr][2 * m2], row_plan[cls][rr][2 * m2 + 1]
                    t0 = cb_ref[h, i0] if ok0 else masked
                    t1 = cb_ref[h, i1] if ok1 else masked
                    bias_ref[h, cls, rr * GRID_W:(rr + 1) * GRID_W, m2 * LANES:(m2 + 1) * LANES] = (
                        jnp.where(lane_t < GRID_W, t0, t1))

    def kprep(i, c):
        s0 = pl.multiple_of(i * ch, ch)
        kf = k_ref[pl.ds(s0, ch), :].astype(F32)
        kn_scr[pl.ds(s0, ch), :] = _head_rms(kf, kw_ref[...], bd).astype(BF16)
        return c

    lax.fori_loop(0, s // ch, kprep, 0)
    lane = lax.broadcasted_iota(I32, (rq, LANES), 1)

    def blk(rb, c):
        q0 = pl.multiple_of(rb * rq, rq)
        w0 = pl.multiple_of(jnp.clip(rb * r_blk - NA_KH // 2, 0, rows - w_blk) * GRID_W, GRID_W)
        cls = jnp.where(rb == 0, 0, jnp.where(rb == nrb - 1, 2, 1))
        qn = _head_rms(q_ref[pl.ds(q0, rq), :].astype(F32), qw_ref[...], bd) * (HEAD_DIM ** -0.5)
        kw = kn_scr[pl.ds(w0, wk), :]
        vw = v_ref[pl.ds(w0, wk), :]
        outs = []
        for h in range(2):
            qh = jnp.where((lane // HEAD_DIM) == h, qn, 0.0).astype(BF16)
            lg = lax.dot_general(qh, kw, NT_DIMS, preferred_element_type=F32)
            lg = lg + bias_ref[h, cls].astype(F32)
            m = jnp.max(lg, axis=-1, keepdims=True)
            p = jnp.exp(lg - m)
            den = jnp.sum(p, axis=-1, keepdims=True)
            outs.append(jnp.dot(p.astype(BF16), vw, preferred_element_type=F32) * (1.0 / den))
        o_ref[pl.ds(q0, rq), :] = jnp.where(lane < HEAD_DIM, outs[0], outs[1]).astype(BF16)
        return c

    lax.fori_loop(0, nrb, blk, 0, unroll=2)


def _na_col_bias(rpb):
    depth = rpb.shape[0]
    col = np.arange(GRID_W)
    cstart = np.clip(col - NA_KW // 2, 0, GRID_W - NA_KW)
    valid_col = (col[None, :] >= cstart[:, None]) & (col[None, :] < cstart[:, None] + NA_KW)
    bj = np.clip(col[None, :] - col[:, None] + NA_KW - 1, 0, 2 * NA_KW - 2)
    sel_col = (bj[:, :, None] == np.arange(2 * NA_KW - 1)) & valid_col[:, :, None]
    t = jnp.einsum('lhij,ckj->lhick', rpb, jnp.asarray(sel_col, F32), precision=lax.Precision.HIGHEST)
    t = jnp.where(jnp.asarray(valid_col)[None, None, None], t, NEG)
    t = jnp.concatenate([t, t], axis=-1)
    return t.reshape(depth, NA_HEADS // 2, 2, 2 * NA_KH - 1, GRID_W, LANES).astype(BF16)


def _na_attention(u3, qw, kw, col_bias, bd, r_blk, w_blk):
    b, s, _ = u3.shape
    rows = s // GRID_W
    rq, wk = r_blk * GRID_W, w_blk * GRID_W
    body = functools.partial(_na_body, rq=rq, wk=wk, rows=rows, r_blk=r_blk, w_blk=w_blk,
                             row_plan=_na_row_plan(r_blk, w_blk, rows))
    return pl.pallas_call(
        body,
        grid=(b, NA_HEADS // 2),
        in_specs=[pl.BlockSpec((None, s, LANES), lambda i, h: (i, 0, U_NAQ // LANES + h)),
                  pl.BlockSpec((None, s, LANES), lambda i, h: (i, 0, U_NAK // LANES + h)),
                  pl.BlockSpec((None, s, LANES), lambda i, h: (i, 0, U_NAV // LANES + h)),
                  pl.BlockSpec((1, LANES), lambda i, h: (0, 0)),
                  pl.BlockSpec((1, LANES), lambda i, h: (0, 0)),
                  pl.BlockSpec((None, 2, 2 * NA_KH - 1, GRID_W, LANES), lambda i, h: (h, 0, 0, 0, 0)),
                  pl.BlockSpec((LANES, LANES), lambda i, h: (0, 0))],
        out_specs=pl.BlockSpec((None, s, LANES), lambda i, h: (i, 0, h)),
        out_shape=SDS((b, s, NA_HEADS * HEAD_DIM), BF16),
        scratch_shapes=[pltpu.VMEM((s, LANES), BF16), pltpu.VMEM((2, 3, rq, wk), BF16)],
        compiler_params=_cparams(("parallel", "parallel")),
        name="na2d",
    )(u3, u3, u3, qw, kw, col_bias, bd)


def _rope(xf, cos, sin):
    lane = lax.broadcasted_iota(I32, xf.shape, 1)
    half = ROPE_DIM // 2
    partner = jnp.where((lane % HEAD_DIM) < half, pltpu.roll(xf, LANES - half, 1), pltpu.roll(xf, half, 1))
    return xf * cos + partner * sin


def _wa_body(sink_ref, q_ref, k_ref, v_ref, cos_ref, sin_ref, qw_ref, kw_ref, bd_ref, o_ref, kd_scr, vd_scr,
             *, qb, wkb):
    s = k_ref.shape[0]
    kvh = pl.program_id(1)
    n = pl.program_id(2)
    bd = bd_ref[...]
    ch = min(512, s)

    @pl.when(n == 0)
    def _():
        lane = lax.broadcasted_iota(I32, (ch, LANES), 1)
        mine = (lane // HEAD_DIM) == kvh

        def prep(i, c):
            s0 = pl.multiple_of(i * ch, ch)
            kf = k_ref[pl.ds(s0, ch), :].astype(F32)
            kf = _rope(_head_rms(kf, kw_ref[...], bd), cos_ref[pl.ds(s0, ch), :], sin_ref[pl.ds(s0, ch), :])
            kd_scr[pl.ds(s0, ch), :] = jnp.where(mine, kf, pltpu.roll(kf, HEAD_DIM, 1)).astype(BF16)
            vf = v_ref[pl.ds(s0, ch), :].astype(F32)
            vd_scr[pl.ds(s0, ch), :] = jnp.where(mine, vf, pltpu.roll(vf, HEAD_DIM, 1)).astype(BF16)
            return c

        lax.fori_loop(0, s // ch, prep, 0)

    lane = lax.broadcasted_iota(I32, (qb, LANES), 1)
    for sub in range(q_ref.shape[0] // qb):
        rs = slice(sub * qb, (sub + 1) * qb)
        q0 = pl.multiple_of(n * q_ref.shape[0] + sub * qb, qb)
        ks = pl.multiple_of(jnp.clip(q0 - WINDOW, 0, s - wkb), LANES)
        cosq = cos_ref[pl.ds(q0, qb), :]
        sinq = sin_ref[pl.ds(q0, qb), :]
        kw = kd_scr[pl.ds(ks, wkb), :]
        vw = vd_scr[pl.ds(ks, wkb), :]
        qpos = q0 + lax.broadcasted_iota(I32, (qb, wkb), 0)
        kpos = ks + lax.broadcasted_iota(I32, (qb, wkb), 1)
        allowed = jnp.abs(kpos - qpos) <= WINDOW
        for half in range(2):
            qf = q_ref[rs, half * LANES:(half + 1) * LANES].astype(F32)
            qn = _rope(_head_rms(qf, qw_ref[...], bd), cosq, sinq) * (HEAD_DIM ** -0.5)
            outs = []
            for gg in range(2):
                qh = jnp.where((lane // HEAD_DIM) == gg, qn, 0.0).astype(BF16)
                lg = lax.dot_general(qh, kw, NT_DIMS, preferred_element_type=F32)
                lg = jnp.where(allowed, lg, NEG)
                sk = sink_ref[0, kvh * 4 + half * 2 + gg]
                m = jnp.maximum(jnp.max(lg, axis=-1, keepdims=True), sk)
                p = jnp.exp(lg - m)
                den = jnp.sum(p, axis=-1, keepdims=True) + jnp.exp(sk - m)
                outs.append(jnp.dot(p.astype(BF16), vw, preferred_element_type=F32) * (1.0 / den))
            o_ref[rs, half * LANES:(half + 1) * LANES] = jnp.where(lane < HEAD_DIM, outs[0], outs[1]).astype(BF16)


def _wa_attention(u3, sink, cos_t, sin_t, qw, kw, bd, qb, qps):
    b, s, _ = u3.shape
    wkb = qb + 2 * WINDOW
    qrows = qb * qps
    body = functools.partial(_wa_body, qb=qb, wkb=wkb)
    return pl.pallas_call(
        body,
        grid=(b, WA_KV_HEADS, s // qrows),
        in_specs=[pl.BlockSpec(memory_space=pltpu.SMEM),
                  pl.BlockSpec((None, qrows, 2 * LANES), lambda i, h, n: (i, n, U_WAQ // (2 * LANES) + h)),
                  pl.BlockSpec((None, s, LANES), lambda i, h, n: (i, 0, U_WAK // LANES)),
                  pl.BlockSpec((None, s, LANES), lambda i, h, n: (i, 0, U_WAV // LANES)),
                  pl.BlockSpec((s, LANES), lambda i, h, n: (0, 0)),
                  pl.BlockSpec((s, LANES), lambda i, h, n: (0, 0)),
                  pl.BlockSpec((1, LANES), lambda i, h, n: (0, 0)),
                  pl.BlockSpec((1, LANES), lambda i, h, n: (0, 0)),
                  pl.BlockSpec((LANES, LANES), lambda i, h, n: (0, 0))],
        out_specs=pl.BlockSpec((None, qrows, 2 * LANES), lambda i, h, n: (i, n, h)),
        out_shape=SDS((b, s, WA_Q_HEADS * HEAD_DIM), BF16),
        scratch_shapes=[pltpu.VMEM((s, LANES), BF16), pltpu.VMEM((s, LANES), BF16)],
        compiler_params=_cparams(("parallel", "arbitrary", "arbitrary")),
        name="swa_gqa",
    )(sink, u3, u3, u3, cos_t, sin_t, qw, kw, bd)


def _rope_tables(s):
    half = ROPE_DIM // 2
    inv_freq = 1.0 / (ROPE_THETA ** (jnp.arange(half, dtype=F32) * 2.0 / ROPE_DIM))
    ang = jnp.arange(s, dtype=F32)[:, None] * inv_freq[None, :]
    cos, sin = jnp.cos(ang), jnp.sin(ang)
    ones = jnp.ones((s, HEAD_DIM - ROPE_DIM), F32)
    cos_h = jnp.concatenate([cos, cos, ones], axis=1)
    sin_h = jnp.concatenate([-sin, sin, 0.0 * ones], axis=1)
    return jnp.tile(cos_h, (1, LANES // HEAD_DIM)), jnp.tile(sin_h, (1, LANES // HEAD_DIM))


def _ssd_body(*refs, reverse, nblk, cpb, final):
    if final:
        (xp_ref, xc_ref, xn_ref, dt_ref, cw_ref, cb_ref, dtb_ref, a_ref, eh_ref,
         yf_ref, z_ref, dv_ref, nw_ref, o_ref, st_scr, cv_scr, g_scr) = refs
    else:
        (xp_ref, xc_ref, xn_ref, dt_ref, cw_ref, cb_ref, dtb_ref, a_ref, eh_ref,
         o_ref, st_scr, cv_scr) = refs
        yf_ref = z_ref = dv_ref = nw_ref = g_scr = None
    bi = pl.program_id(1)
    blk = (nblk - 1 - bi) if reverse else bi
    L = SSM_CHUNK
    rows = cpb * L

    @pl.when(bi == 0)
    def _():
        st_scr[...] = jnp.zeros_like(st_scr)

    has_prev = jnp.where(blk > 0, 1.0, 0.0)
    has_next = jnp.where(blk < nblk - 1, 1.0, 0.0)
    cv_scr[0:8, :] = xp_ref[SLOT_ALIGN - 8:SLOT_ALIGN, :].astype(F32) * has_prev
    cv_scr[8:8 + rows, :] = xc_ref[...].astype(F32)
    cv_scr[8 + rows:16 + rows, :] = xn_ref[0:8, :].astype(F32) * has_next
    for ii in range(cpb):
        ci = (cpb - 1 - ii) if reverse else ii
        _ssd_chunk(ci * L, dt_ref, cw_ref, cb_ref, dtb_ref, a_ref, eh_ref, yf_ref, z_ref, dv_ref, nw_ref,
                   o_ref, st_scr, cv_scr, g_scr, reverse=reverse, final=final)


def _ssd_chunk(r0, dt_ref, cw_ref, cb_ref, dtb_ref, a_ref, eh_ref, yf_ref, z_ref, dv_ref, nw_ref,
               o_ref, st_scr, cv_scr, g_scr, *, reverse, final):
    L = SSM_CHUNK
    sw = SSM_HEADS * HEAD_DIM
    rs = slice(r0, r0 + L)
    left = (SSM_CONV - 1) // 2
    acc = cb_ref[...] + cw_ref[0:1, :] * cv_scr[pl.ds(r0 + 8 - left, L), :]
    for k in range(1, SSM_CONV):
        acc = acc + cw_ref[k:k + 1, :] * cv_scr[pl.ds(r0 + 8 - left + k, L), :]
    xa = acc * _sigmoid(acc)
    xs = xa[:, 0:sw]

    raw = dt_ref[rs, :] + dtb_ref[...]
    dtv = jnp.maximum(raw, 0.0) + jnp.log1p(jnp.exp(-jnp.abs(raw)))
    da = dtv * a_ref[...]
    ri = lax.broadcasted_iota(I32, (L, L), 0)
    cj = lax.broadcasted_iota(I32, (L, L), 1)
    keep = (cj >= ri) if reverse else (cj <= ri)
    tri = jnp.where(keep, 1.0, 0.0).astype(BF16)
    d1, d2, d3 = _split3(da)
    cum = (jnp.dot(tri, d1, preferred_element_type=F32) + jnp.dot(tri, d2, preferred_element_type=F32)
           + jnp.dot(tri, d3, preferred_element_type=F32))
    cum_t = cum.T
    eh = eh_ref[...]
    c1, c2, c3 = _split3(cum)
    cum_x = (jnp.dot(c1, eh, preferred_element_type=F32) + jnp.dot(c2, eh, preferred_element_type=F32)
             + jnp.dot(c3, eh, preferred_element_type=F32))
    t1, t2, t3 = _split3(dtv)
    dt_x = (jnp.dot(t1, eh, preferred_element_type=F32) + jnp.dot(t2, eh, preferred_element_type=F32)
            + jnp.dot(t3, eh, preferred_element_type=F32))
    tot_x = cum_x[0:1, :] if reverse else cum_x[L - 1:L, :]
    xc = xs * dt_x
    xdec = (xc * jnp.exp(tot_x - cum_x)).astype(BF16)
    xcb = xc.astype(BF16)
    ecum = jnp.exp(cum_x)
    etot = jnp.exp(tot_x)
    lane = lax.broadcasted_iota(I32, (L, LANES), 1)
    doff = SSM_HEADS if reverse else 0
    for g in range(2):
        bg = xa[:, sw + g * SSM_STATE: sw + (g + 1) * SSM_STATE]
        cg = xa[:, sw + 2 * SSM_STATE + g * SSM_STATE: sw + 2 * SSM_STATE + (g + 1) * SSM_STATE]
        cgb = cg.astype(BF16)
        cb = lax.dot_general(cgb, bg.astype(BF16), NT_DIMS, preferred_element_type=F32)
        bgt = bg.T.astype(BF16)
        for pp in range(2):
            p = g * 2 + pp
            sl = slice(p * LANES, (p + 1) * LANES)
            ys = []
            for hh in range(2):
                j = doff + 2 * p + hh
                seg = cum[:, j:j + 1] - cum_t[j:j + 1, :]
                dec = jnp.exp(jnp.where(keep, seg, NEG))
                ys.append(jnp.dot((cb * dec).astype(BF16), xcb[:, sl], preferred_element_type=F32))
            y = jnp.where(lane < HEAD_DIM, ys[0], ys[1])
            prev = st_scr[p]
            y = y + jnp.dot(cgb, prev.astype(BF16), preferred_element_type=F32) * ecum[:, sl]
            st_scr[p] = etot[:, sl] * prev + jnp.dot(bgt, xdec[:, sl], preferred_element_type=F32)
            if final:
                y = y + yf_ref[rs, sl] + xs[:, sl] * dv_ref[:, sl]
                zf = z_ref[rs, sl].astype(F32)
                g_scr[:, sl] = y * (zf * _sigmoid(zf))
            else:
                o_ref[rs, sl] = y
    if final:
        gated = g_scr[...]
        ms = jnp.mean(gated * gated, axis=-1, keepdims=True)
        o_ref[rs, :] = (gated * lax.rsqrt(ms + EPS) * nw_ref[...]).astype(BF16)


def _ssd_pass(u3, dt3, cw, cb, dtb, a_vec, eh, extra, reverse, cpb):
    b, s, _ = u3.shape
    L = SSM_CHUNK
    rows = cpb * L
    nblk = s // rows
    hpb = rows // SLOT_ALIGN
    sw = SSM_HEADS * HEAD_DIM
    final = extra is not None

    def cidx(ci):
        return (nblk - 1 - ci) if reverse else ci

    xw = 2 * sw
    in_specs = [pl.BlockSpec((None, SLOT_ALIGN, xw),
                             lambda i, ci: (i, jnp.maximum(cidx(ci) * hpb - 1, 0), U_XBC // xw)),
                pl.BlockSpec((None, rows, xw), lambda i, ci: (i, cidx(ci), U_XBC // xw)),
                pl.BlockSpec((None, SLOT_ALIGN, xw),
                             lambda i, ci: (i, jnp.minimum((cidx(ci) + 1) * hpb, nblk * hpb - 1), U_XBC // xw)),
                pl.BlockSpec((None, rows, LANES), lambda i, ci: (i, cidx(ci), 0)),
                pl.BlockSpec((8, xw), lambda i, ci: (0, 0)),
                pl.BlockSpec((1, xw), lambda i, ci: (0, 0)),
                pl.BlockSpec((1, LANES), lambda i, ci: (0, 0)),
                pl.BlockSpec((1, LANES), lambda i, ci: (0, 0)),
                pl.BlockSpec((LANES, sw), lambda i, ci: (0, 0))]
    args = [u3, u3, u3, dt3, cw, cb, dtb, a_vec, eh]
    if final:
        y_f, dvec, nw = extra
        in_specs += [pl.BlockSpec((None, rows, sw), lambda i, ci: (i, cidx(ci), 0)),
                     pl.BlockSpec((None, rows, sw), lambda i, ci: (i, cidx(ci), U_Z // sw)),
                     pl.BlockSpec((1, sw), lambda i, ci: (0, 0)),
                     pl.BlockSpec((1, sw), lambda i, ci: (0, 0))]
        args += [y_f, u3, dvec, nw]
    scratch = [pltpu.VMEM((4, SSM_STATE, LANES), F32), pltpu.VMEM((rows + 16, xw), F32)]
    if final:
        scratch.append(pltpu.VMEM((L, sw), F32))
    body = functools.partial(_ssd_body, reverse=reverse, nblk=nblk, cpb=cpb, final=final)
    return pl.pallas_call(
        body,
        grid=(b, nblk),
        in_specs=in_specs,
        out_specs=pl.BlockSpec((None, rows, sw), lambda i, ci: (i, cidx(ci), 0)),
        out_shape=SDS((b, s, sw), BF16 if final else F32),
        scratch_shapes=scratch,
        compiler_params=_cparams(("parallel", "arbitrary")),
        name="ssd_bwd" if reverse else "ssd_fwd",
    )(*args)


def _conf_body(xp_ref, xc_ref, xn_ref, w_ref, b_ref, lnw_ref, lnb_ref, o_ref, scr, ph_scr, *, tc, nt):
    i = pl.program_id(1)
    cwid = CONF_WIDTH
    halo = 16

    def glu(v):
        return v[:, 0:cwid].astype(F32) * _sigmoid(v[:, cwid:2 * cwid].astype(F32))

    scr[0:halo, :] = glu(xp_ref[...]) * jnp.where(i > 0, 1.0, 0.0)
    scr[halo:halo + tc, :] = glu(xc_ref[...])
    scr[halo + tc:2 * halo + tc, :] = glu(xn_ref[...]) * jnp.where(i < nt - 1, 1.0, 0.0)
    first = halo - (CONF_KERNEL - 1) // 2
    acc = b_ref[...]
    for ph in range(SUBLANES):
        part = None
        for row in range(ph, first + CONF_KERNEL, SUBLANES):
            k = row - first
            if 0 <= k < CONF_KERNEL:
                term = w_ref[k:k + 1, :] * scr[pl.ds(row - ph, tc + SUBLANES), :]
                part = term if part is None else part + term
        if ph == 0:
            acc = acc + part[0:tc]
        else:
            ph_scr[...] = part
            acc = acc + ph_scr[pl.ds(ph, tc), :]
    mu = jnp.mean(acc, axis=-1, keepdims=True)
    cen = acc - mu
    var = jnp.mean(cen * cen, axis=-1, keepdims=True)
    y = cen * lax.rsqrt(var + EPS) * lnw_ref[...] + lnb_ref[...]
    o_ref[...] = (y * _sigmoid(y)).astype(BF16)


def _conformer(u3, w, bvec, lnw, lnb, tc):
    b, s, _ = u3.shape
    nt = s // tc
    hpb = tc // SLOT_ALIGN
    cw2 = 2 * CONF_WIDTH
    body = functools.partial(_conf_body, tc=tc, nt=nt)
    return pl.pallas_call(
        body,
        grid=(b, nt),
        in_specs=[pl.BlockSpec((None, SLOT_ALIGN, cw2),
                               lambda i, t: (i, jnp.maximum(t * hpb - 1, 0), U_CONF // cw2)),
                  pl.BlockSpec((None, tc, cw2), lambda i, t: (i, t, U_CONF // cw2)),
                  pl.BlockSpec((None, SLOT_ALIGN, cw2),
                               lambda i, t: (i, jnp.minimum((t + 1) * hpb, nt * hpb - 1), U_CONF // cw2)),
                  pl.BlockSpec((32, CONF_WIDTH), lambda i, t: (0, 0)),
                  pl.BlockSpec((1, CONF_WIDTH), lambda i, t: (0, 0)),
                  pl.BlockSpec((1, CONF_WIDTH), lambda i, t: (0, 0)),
                  pl.BlockSpec((1, CONF_WIDTH), lambda i, t: (0, 0))],
        out_specs=pl.BlockSpec((None, tc, CONF_WIDTH), lambda i, t: (i, t, 0)),
        out_shape=SDS((b, s, CONF_WIDTH), BF16),
        scratch_shapes=[pltpu.VMEM((tc + 32, CONF_WIDTH), F32), pltpu.VMEM((tc + SUBLANES, CONF_WIDTH), F32)],
        compiler_params=_cparams(("parallel", "parallel")),
        name="conformer",
    )(u3, u3, u3, w, bvec, lnw, lnb)


def _outproj_body(oa_ref, ob_ref, oc_ref, od_ref, w_ref, x_ref, fw_ref, wr_ref, xo_ref, h_ref, lg_ref):
    mixed = jnp.concatenate([oa_ref[...], ob_ref[...], oc_ref[...], od_ref[...]], axis=1)
    acc = x_ref[...] + jnp.dot(mixed, w_ref[...], preferred_element_type=F32)
    xo_ref[...] = acc
    ms = jnp.mean(acc * acc, axis=-1, keepdims=True)
    hf = acc * lax.rsqrt(ms + EPS) * fw_ref[...]
    hb = hf.astype(BF16)
    h_ref[...] = hb
    lo = (hf - hb.astype(F32)).astype(BF16)
    wr = wr_ref[...]
    both = jnp.dot(hb, wr, preferred_element_type=F32)
    lg = both[:, 0:LANES] + both[:, LANES:2 * LANES] + jnp.dot(lo, wr[:, 0:LANES], preferred_element_type=F32)
    lg_ref[...] = lg.T[0:N_EXPERTS, :]


def _out_proj(o_a, o_b, o_c, o_d, w_out, layer, x3, fw, wr, tm):
    b, s, _ = x3.shape
    gw = 512
    mix_spec = pl.BlockSpec((None, tm, gw), lambda i, t: (i, t, 0))
    return pl.pallas_call(
        _outproj_body,
        grid=(b, s // tm),
        in_specs=[mix_spec, mix_spec, mix_spec, mix_spec,
                  pl.BlockSpec((None, D_MODEL, D_MODEL), lambda i, t: (layer, 0, 0)),
                  pl.BlockSpec((None, tm, D_MODEL), lambda i, t: (i, t, 0)),
                  pl.BlockSpec((1, D_MODEL), lambda i, t: (0, 0)),
                  pl.BlockSpec((D_MODEL, 2 * LANES), lambda i, t: (0, 0))],
        out_specs=[pl.BlockSpec((None, tm, D_MODEL), lambda i, t: (i, t, 0)),
                   pl.BlockSpec((None, tm, D_MODEL), lambda i, t: (i, t, 0)),
                   pl.BlockSpec((None, N_EXPERTS, tm), lambda i, t: (i, 0, t))],
        out_shape=[SDS((b, s, D_MODEL), F32), SDS((b, s, D_MODEL), BF16), SDS((b, N_EXPERTS, s), F32)],
        compiler_params=_cparams(("parallel", "parallel")),
        name="out_proj",
    )(o_a, o_b, o_c, o_d, w_out, x3, fw, wr)


def _route_body(lg_ref, posm_ref, gate_ref, offs_ref, m_scr, c_scr, *, cap, tk):
    e, s = lg_ref.shape
    cb = 256
    nblk = s // cb
    lg = lg_ref[...]
    mx = jnp.max(lg, axis=0, keepdims=True)
    ex = jnp.exp(lg - mx)
    aff = ex / jnp.sum(ex, axis=0, keepdims=True)
    gate_ref[...] = aff
    bits = pltpu.bitcast(aff, I32)

    def search(i, v):
        cand = v | jnp.left_shift(jnp.int32(1), 30 - i)
        cnt = jnp.sum(jnp.where(bits >= cand, 1.0, 0.0), axis=1, keepdims=True)
        return jnp.where(cnt >= cap, cand, v)

    thr = lax.fori_loop(0, 31, search, jnp.zeros((e, 1), I32))
    gt = bits > thr
    eq = bits == thr
    need = cap - jnp.sum(jnp.where(gt, 1.0, 0.0), axis=1, keepdims=True)
    ri = lax.broadcasted_iota(I32, (cb, cb), 0)
    cj = lax.broadcasted_iota(I32, (cb, cb), 1)
    tri = jnp.where(ri <= cj, 1.0, 0.0).astype(BF16)

    def cumsum_into_c():
        def blk(i, carry):
            o = pl.multiple_of(i * cb, cb)
            inc = jnp.dot(m_scr[:, pl.ds(o, cb)].astype(BF16), tri, preferred_element_type=F32) + carry
            c_scr[:, pl.ds(o, cb)] = inc
            return inc[:, cb - 1:cb]
        lax.fori_loop(0, nblk, blk, jnp.zeros((e, 1), F32))

    eqf = jnp.where(eq, 1.0, 0.0)
    m_scr[...] = eqf
    cumsum_into_c()
    sel = gt | (eq & ((c_scr[...] - eqf) < need))
    self_f = jnp.where(sel, 1.0, 0.0)
    m_scr[...] = self_f
    cumsum_into_c()
    posm_ref[...] = jnp.where(sel, (c_scr[...] - self_f).astype(I32), -1)
    lane = lax.broadcasted_iota(I32, (e, LANES), 1)
    offs = jnp.zeros((e, LANES), I32)
    for c in range(1, s // tk + 1):
        offs = jnp.where(lane == c, c_scr[:, c * tk - 1:c * tk].astype(I32), offs)
    offs_ref[...] = offs


def _route(logits_t, cap, tk):
    b, e, s = logits_t.shape
    body = functools.partial(_route_body, cap=cap, tk=tk)
    return pl.pallas_call(
        body,
        grid=(b,),
        in_specs=[pl.BlockSpec((None, e, s), lambda i: (i, 0, 0))],
        out_specs=[pl.BlockSpec((None, e, s), lambda i: (i, 0, 0)),
                   pl.BlockSpec((None, e, s), lambda i: (i, 0, 0)),
                   pl.BlockSpec((None, e, LANES), lambda i: (i, 0, 0))],
        out_shape=[SDS((b, e, s), I32), SDS((b, e, s), F32), SDS((b, e, LANES), I32)],
        scratch_shapes=[pltpu.VMEM((e, s), F32), pltpu.VMEM((e, s), F32)],
        compiler_params=_cparams(("parallel",)),
        name="route",
    )(logits_t)


def _window_plan(offs_ref, row0, ci, n_exp, row_stride, mp):
    starts, npieces = [], 0
    for k in range(n_exp):
        at = (row0 + k) * row_stride + ci
        off = offs_ref[at]
        n = offs_ref[at + 1] - off
        start = (off // SLOT_ALIGN) * SLOT_ALIGN
        starts.append(start)
        npieces = jnp.maximum(npieces, jnp.where(n > 0, (off - start + n + mp - 1) // mp, 0))
    return starts, npieces


def _gather_body(offs_ref, posm_ref, h_ref, o_ref, *, eg, mp, cap, row_stride):
    bi, gi, ci = pl.program_id(0), pl.program_id(1), pl.program_id(2)
    tk = h_ref.shape[0]

    @pl.when(ci == 0)
    def _():
        o_ref[...] = jnp.zeros_like(o_ref)

    starts, npieces = _window_plan(offs_ref, bi * N_EXPERTS + gi * eg, ci, eg, row_stride, mp)
    riota = lax.broadcasted_iota(I32, (mp, tk), 0)

    def piece(p, carry):
        bands = [jnp.where(riota == posm_ref[k] - (starts[k] + p * mp), 1.0, 0.0).astype(BF16) for k in range(eg)]
        res = jnp.dot(jnp.concatenate(bands, axis=0), h_ref[...], preferred_element_type=F32)
        for k in range(eg):
            dst = pl.multiple_of(jnp.minimum(starts[k] + p * mp, cap), SLOT_ALIGN)
            o_ref[k, pl.ds(dst, mp), :] = o_ref[k, pl.ds(dst, mp), :] + res[k * mp:(k + 1) * mp].astype(BF16)
        return carry

    lax.fori_loop(0, npieces, piece, 0)


def _gather(offs_flat, posm4, h3, cap, tk, eg, mp):
    b, e, _, s = posm4.shape
    nch = s // tk
    body = functools.partial(_gather_body, eg=eg, mp=mp, cap=cap, row_stride=LANES)
    grid_spec = pltpu.PrefetchScalarGridSpec(
        num_scalar_prefetch=1,
        grid=(b, e // eg, nch),
        in_specs=[pl.BlockSpec((None, eg, 1, tk), lambda bi, gi, ci, m: (bi, gi, 0, ci)),
                  pl.BlockSpec((None, tk, D_MODEL), lambda bi, gi, ci, m: (bi, ci, 0))],
        out_specs=pl.BlockSpec((None, eg, cap + mp, D_MODEL), lambda bi, gi, ci, m: (bi, gi, 0, 0)),
    )
    return pl.pallas_call(
        body,
        grid_spec=grid_spec,
        out_shape=SDS((b, e, cap + mp, D_MODEL), BF16),
        compiler_params=_cparams(("parallel", "parallel", "arbitrary")),
        name="moe_gather",
    )(offs_flat, posm4, h3)


def _ffn_body(x_ref, wg_ref, wu_ref, wd_ref, o_ref, acc):
    f = pl.program_id(2)
    x = x_ref[...]
    g = jnp.dot(x, wg_ref[...].astype(BF16), preferred_element_type=F32)
    u = jnp.dot(x, wu_ref[...].astype(BF16), preferred_element_type=F32)
    hid = (g * _sigmoid(g) * u).astype(BF16)

    @pl.when(f == 0)
    def _():
        acc[...] = jnp.zeros_like(acc)

    acc[...] += jnp.dot(hid, wd_ref[...].astype(BF16), preferred_element_type=F32)

    @pl.when(f == pl.num_programs(2) - 1)
    def _():
        o_ref[...] = acc[...].astype(BF16)


def _expert_ffn(xg, w_gate, w_up, w_down, layer, cap, tf):
    b, e = xg.shape[0], xg.shape[1]
    return pl.pallas_call(
        _ffn_body,
        grid=(b, e, EXPERT_FF // tf),
        in_specs=[pl.BlockSpec((None, None, cap, D_MODEL), lambda bi, ei, f: (bi, ei, 0, 0)),
                  pl.BlockSpec((None, None, D_MODEL, tf), lambda bi, ei, f: (layer, ei, 0, f)),
                  pl.BlockSpec((None, None, D_MODEL, tf), lambda bi, ei, f: (layer, ei, 0, f)),
                  pl.BlockSpec((None, None, tf, D_MODEL), lambda bi, ei, f: (layer, ei, f, 0))],
        out_specs=pl.BlockSpec((None, None, cap, D_MODEL), lambda bi, ei, f: (bi, ei, 0, 0)),
        out_shape=SDS((b, e, cap, D_MODEL), BF16),
        scratch_shapes=[pltpu.VMEM((cap, D_MODEL), F32)],
        compiler_params=_cparams(("parallel", "parallel", "arbitrary")),
        name="expert_ffn",
    )(xg, w_gate, w_up, w_down)


def _combine_body(offs_ref, x_ref, posm_ref, gate_ref, y_hbm, o_ref, ybuf, sem, *, mp, cap, row_stride):
    bi, ci = pl.program_id(0), pl.program_id(1)
    nch = pl.num_programs(1)
    step = bi * nch + ci
    slot = step % 2
    tk = x_ref.shape[0]
    riota = lax.broadcasted_iota(I32, (mp, tk), 0)

    def window_copies(b, starts, p, to_slot):
        copies = []
        for k in range(N_EXPERTS):
            src = pl.multiple_of(jnp.minimum(starts[k] + p * mp, cap - mp), SLOT_ALIGN)
            copies.append(pltpu.make_async_copy(y_hbm.at[b, k, pl.ds(src, mp), :],
                                                ybuf.at[to_slot, pl.ds(k * mp, mp), :], sem.at[to_slot]))
        return copies

    starts, npieces = _window_plan(offs_ref, bi * N_EXPERTS, ci, N_EXPERTS, row_stride, mp)

    @pl.when(step == 0)
    def _():
        for cp in window_copies(bi, starts, 0, slot):
            cp.start()

    @pl.when(step + 1 < pl.num_programs(0) * nch)
    def _():
        nb = (step + 1) // nch
        nc = (step + 1) % nch
        nstarts, _ = _window_plan(offs_ref, nb * N_EXPERTS, nc, N_EXPERTS, row_stride, mp)
        for cp in window_copies(nb, nstarts, 0, 1 - slot):
            cp.start()

    def contribution(p):
        bands = []
        for k in range(N_EXPERTS):
            pos = posm_ref[k]
            lo = starts[k] + p * mp
            row = jnp.where((pos >= lo) & (pos < lo + mp), pos - jnp.minimum(lo, cap - mp), -1)
            bands.append(jnp.where(riota == row, gate_ref[k], 0.0).astype(BF16))
        w = jnp.concatenate(bands, axis=0)
        return lax.dot_general(w, ybuf[slot], TN_DIMS, preferred_element_type=F32)

    for cp in window_copies(bi, starts, 0, slot):
        cp.wait()
    o_ref[...] = x_ref[...] + contribution(0)

    def more(p, carry):
        copies = window_copies(bi, starts, p, slot)
        for cp in copies:
            cp.start()
        for cp in copies:
            cp.wait()
        o_ref[...] += contribution(p)
        return carry

    lax.fori_loop(1, npieces, more, 0)


def _combine(offs_flat, x3, posm4, gate4, y4, cap, tk, mp):
    b, s, _ = x3.shape
    body = functools.partial(_combine_body, mp=mp, cap=cap, row_stride=LANES)
    route_spec = pl.BlockSpec((None, N_EXPERTS, 1, tk), lambda bi, ci, m: (bi, 0, 0, ci))
    grid_spec = pltpu.PrefetchScalarGridSpec(
        num_scalar_prefetch=1,
        grid=(b, s // tk),
        in_specs=[pl.BlockSpec((None, tk, D_MODEL), lambda bi, ci, m: (bi, ci, 0)),
                  route_spec, route_spec,
                  pl.BlockSpec(memory_space=pl.ANY)],
        out_specs=pl.BlockSpec((None, tk, D_MODEL), lambda bi, ci, m: (bi, ci, 0)),
        scratch_shapes=[pltpu.VMEM((2, N_EXPERTS * mp, D_MODEL), BF16), pltpu.SemaphoreType.DMA((2,))],
    )
    return pl.pallas_call(
        body,
        grid_spec=grid_spec,
        out_shape=SDS((b, s, D_MODEL), F32),
        compiler_params=_cparams(("arbitrary", "arbitrary")),
        name="moe_combine",
    )(offs_flat, x3, posm4, gate4, y4)


def _repack_w_in(w_in):
    na_w = NA_HEADS * HEAD_DIM
    o = np.cumsum([0, na_w, na_w, na_w, 512, 128, 128, 512, 1024, 16, 1024])
    naq, nak, nav, waq, wak, wav, z, xbc, dt, conf = [w_in[..., o[i]:o[i + 1]] for i in range(10)]
    w_main = jnp.concatenate([xbc, conf, naq, nak, nav, waq, z, wak, wav], axis=-1).astype(BF16)
    w_dt = jnp.pad(dt, ((0, 0), (0, 0), (0, LANES - dt.shape[-1]))).astype(BF16)
    return w_main, w_dt


def _pad_lanes(v, width=LANES):
    return jnp.pad(v, [(0, 0)] * (v.ndim - 1) + [(0, width - v.shape[-1])])


def kernel(x, mix_norm_w, w_in, na_q_norm, na_k_norm, na_rpb, wa_q_norm, wa_k_norm, wa_sink, ssm_conv_w, ssm_conv_b, ssm_dt_bias, ssm_a_log, ssm_d, ssm_norm_w, conf_dw_w, conf_dw_b, conf_ln_w, conf_ln_b, w_out, ffn_norm_w, w_router, w_gate, w_up, w_down):
    b, s, d = x.shape
    depth = w_in.shape[0]
    rows = s // GRID_W
    cap = EC_CAPACITY * s // N_EXPERTS
    r_blk, w_blk = 4, 12
    tk = min(512, s)
    mp = min(128, cap)

    w_main, w_dt = _repack_w_in(w_in)
    w_out_b = w_out.astype(BF16)
    wr_pad = _pad_lanes(w_router)
    wr_hi = wr_pad.astype(BF16)
    wr_cat = jnp.concatenate([wr_hi, (wr_pad - wr_hi.astype(F32)).astype(BF16)], axis=-1)
    cos_t, sin_t = _rope_tables(s)
    bd = jnp.asarray(np.kron(np.eye(LANES // HEAD_DIM), np.full((HEAD_DIM, HEAD_DIM), 1.0 / HEAD_DIM)), BF16)
    eh_np = np.zeros((2, LANES, SSM_HEADS * HEAD_DIM), np.float32)
    for dr in range(2):
        for h in range(SSM_HEADS):
            eh_np[dr, dr * SSM_HEADS + h, h * HEAD_DIM:(h + 1) * HEAD_DIM] = 1.0
    eh = jnp.asarray(eh_np, BF16)
    tile2 = lambda v: jnp.tile(v, (1, LANES // HEAD_DIM))[:, None, :]
    naq_w, nak_w, waq_w, wak_w = tile2(na_q_norm), tile2(na_k_norm), tile2(wa_q_norm), tile2(wa_k_norm)
    dtb = _pad_lanes(ssm_dt_bias.reshape(depth, 1, 2 * SSM_HEADS))
    a_vec = _pad_lanes(-jnp.exp(ssm_a_log.reshape(depth, 1, 2 * SSM_HEADS)))
    dvec = jnp.repeat(ssm_d, HEAD_DIM, axis=-1)[:, None, :]
    conv_w = jnp.pad(ssm_conv_w, ((0, 0), (0, 8 - SSM_CONV), (0, 0)))
    conf_w = jnp.pad(conf_dw_w, ((0, 0), (0, 32 - CONF_KERNEL), (0, 0)))
    col_bias = _na_col_bias(na_rpb)

    for l in range(depth):
        u, dt_raw = _in_proj(x.reshape(b * s, d), mix_norm_w[l][None], w_main, w_dt, l)
        u3 = u.reshape(b, s, U_WIDTH)
        dt3 = dt_raw.reshape(b, s, LANES)
        o_a = _na_attention(u3, naq_w[l], nak_w[l], col_bias[l], bd, r_blk, w_blk)
        o_b = _wa_attention(u3, wa_sink[l][None], cos_t, sin_t, waq_w[l], wak_w[l], bd, 256, 2)
        ssd_args = (u3, dt3, conv_w[l], ssm_conv_b[l][None], dtb[l], a_vec[l])
        y_f = _ssd_pass(*ssd_args, eh[0], None, reverse=False, cpb=4)
        o_c = _ssd_pass(*ssd_args, eh[1], (y_f, dvec[l], ssm_norm_w[l][None]), reverse=True, cpb=4)
        o_d = _conformer(u3, conf_w[l], conf_dw_b[l][None], conf_ln_w[l][None], conf_ln_b[l][None], min(256, s))
        x, h2, logits_t = _out_proj(o_a, o_b, o_c, o_d, w_out_b, l, x, ffn_norm_w[l][None], wr_cat[l], min(512, s))
        posm, gate, offs = _route(logits_t, cap, tk)
        offs_flat = offs.reshape(-1)
        posm4 = posm.reshape(b, N_EXPERTS, 1, s)
        gate4 = gate.reshape(b, N_EXPERTS, 1, s)
        xg = _gather(offs_flat, posm4, h2, cap, tk, 4, mp)
        y4 = _expert_ffn(xg, w_gate, w_up, w_down, l, cap, 256)
        x = _combine(offs_flat, x, posm4, gate4, y4, cap, tk, mp)
    return x
```

```python
import functools

import numpy as np
import jax
import jax.numpy as jnp
from jax import lax
from jax.experimental import pallas as pl
from jax.experimental.pallas import tpu as pltpu

F32 = jnp.float32
BF16 = jnp.bfloat16
I32 = jnp.int32
SDS = jax.ShapeDtypeStruct

D_MODEL = 2048
HEAD_DIM = 64
EPS = 1e-6
NA_HEADS = 8
NA_KH = 8
NA_KW = 16
GRID_W = 64
WA_Q_HEADS = 8
WA_KV_HEADS = 2
WINDOW = 128
ROPE_THETA = 500000.0
ROPE_DIM = HEAD_DIM // 4
SSM_HEADS = 8
SSM_STATE = 128
SSM_CONV = 5
SSM_CHUNK = 128
CONF_WIDTH = 512
CONF_KERNEL = 31
N_EXPERTS = 16
EC_CAPACITY = 2
EXPERT_FF = D_MODEL // 2

U_XBC = 0
U_CONF = 1024
U_NAQ = 2048
U_NAK = 2560
U_NAV = 3072
U_WAQ = 3584
U_Z = 4096
U_WAK = 4608
U_WAV = 4736
U_WIDTH = 4864
LANES = 128
SUBLANES = 8
SLOT_ALIGN = 16
VMEM_LIMIT = 56 * 1024 * 1024

NT_DIMS = (((1,), (1,)), ((), ()))
TN_DIMS = (((0,), (0,)), ((), ()))
NEG = -1e30


def _cparams(sem):
    return pltpu.CompilerParams(dimension_semantics=sem, vmem_limit_bytes=VMEM_LIMIT)


def _split3(v):
    a = v.astype(BF16)
    r = v - a.astype(F32)
    b = r.astype(BF16)
    c = (r - b.astype(F32)).astype(BF16)
    return a, b, c


def _sigmoid(v):
    return 1.0 / (1.0 + jnp.exp(-v))


def _head_rms(xf, w, bd):
    sq = xf * xf
    hi = sq.astype(BF16)
    lo = (sq - hi.astype(F32)).astype(BF16)
    ms = (jnp.dot(hi, bd, preferred_element_type=F32)
          + jnp.dot(lo, bd, preferred_element_type=F32))
    return xf * lax.rsqrt(ms + EPS) * w


def _inproj_body(x_ref, nw_ref, w_ref, wdt_ref, u_ref, dt_ref):
    x = x_ref[...]
    ms = jnp.mean(x * x, axis=-1, keepdims=True)
    h = (x * lax.rsqrt(ms + EPS) * nw_ref[...]).astype(BF16)
    u_ref[...] = jnp.dot(h, w_ref[...], preferred_element_type=F32).astype(BF16)

    @pl.when(pl.program_id(0) == 0)
    def _():
        dt_ref[...] = jnp.dot(h, wdt_ref[...], preferred_element_type=F32)

    @pl.when(pl.program_id(0) != 0)
    def _():
        dt_ref[...] = jnp.zeros_like(dt_ref)


def _in_proj(x2, norm_w, w_main, w_dt, layer, tm=512):
    m = x2.shape[0]
    tn = U_WIDTH // 2
    return pl.pallas_call(
        _inproj_body,
        grid=(U_WIDTH // tn, m // tm),
        in_specs=[pl.BlockSpec((tm, D_MODEL), lambda j, i: (i, 0)),
                  pl.BlockSpec((1, D_MODEL), lambda j, i: (0, 0)),
                  pl.BlockSpec((None, D_MODEL, tn), lambda j, i: (layer, 0, j)),
                  pl.BlockSpec((None, D_MODEL, LANES), lambda j, i: (layer, 0, 0))],
        out_specs=[pl.BlockSpec((tm, tn), lambda j, i: (i, j)),
                   pl.BlockSpec((None, tm, LANES), lambda j, i: (j, i, 0))],
        out_shape=[SDS((m, U_WIDTH), BF16), SDS((U_WIDTH // tn, m, LANES), F32)],
        compiler_params=_cparams(("arbitrary", "arbitrary")),
        name="in_proj",
    )(x2, norm_w, w_main, w_dt)


def _na_row_plan(r_blk, w_blk, rows):
    nrb = rows // r_blk
    plan = []
    for rb in (0, min(1, nrb - 1), nrb - 1):
        w0 = int(np.clip(rb * r_blk - NA_KH // 2, 0, rows - w_blk))
        cls = []
        for rr in range(r_blk):
            r = rb * r_blk + rr
            r0 = int(np.clip(r - NA_KH // 2, 0, rows - NA_KH))
            cls.append(tuple((r0 <= w0 + wi < r0 + NA_KH, w0 + wi - r + NA_KH - 1) for wi in range(w_blk)))
        plan.append(tuple(cls))
    return tuple(plan)


def _na_body(q_ref, k_ref, v_ref, qw_ref, kw_ref, cb_ref, bd_ref, o_ref, kn_scr, bias_ref,
             *, rq, wk, rows, r_blk, w_blk, row_plan):
    s = q_ref.shape[0]
    nrb = rows // r_blk
    bd = bd_ref[...]
    ch = min(512, s)

    lane_t = lax.broadcasted_iota(I32, (GRID_W, LANES), 1)
    masked = jnp.full((GRID_W, LANES), NEG, BF16)
    for h in range(2):
        for cls in range(3):
            for rr in range(r_blk):
                for m2 in range(w_blk // 2):
                    (ok0, i0), (ok1, i1) = row_plan[cls][rr][2 * m2], row_plan[cls][rr][2 * m2 + 1]
                    t0 = cb_ref[h, i0] if ok0 else masked
                    t1 = cb_ref[h, i1] if ok1 else masked
                    bias_ref[h, cls, rr * GRID_W:(rr + 1) * GRID_W, m2 * LANES:(m2 + 1) * LANES] = (
                        jnp.where(lane_t < GRID_W, t0, t1))

    def kprep(i, c):
        s0 = pl.multiple_of(i * ch, ch)
        kf = k_ref[pl.ds(s0, ch), :].astype(F32)
        kn_scr[pl.ds(s0, ch), :] = _head_rms(kf, kw_ref[...], bd).astype(BF16)
        return c

    lax.fori_loop(0, s // ch, kprep, 0)
    lane = lax.broadcasted_iota(I32, (rq, LANES), 1)

    def blk(rb, c):
        q0 = pl.multiple_of(rb * rq, rq)
        w0 = pl.multiple_of(jnp.clip(rb * r_blk - NA_KH // 2, 0, rows - w_blk) * GRID_W, GRID_W)
        cls = jnp.where(rb == 0, 0, jnp.where(rb == nrb - 1, 2, 1))
        qn = _head_rms(q_ref[pl.ds(q0, rq), :].astype(F32), qw_ref[...], bd) * (HEAD_DIM ** -0.5)
        kw = kn_scr[pl.ds(w0, wk), :]
        vw = v_ref[pl.ds(w0, wk), :]
        outs = []
        for h in range(2):
            qh = jnp.where((lane // HEAD_DIM) == h, qn, 0.0).astype(BF16)
            lg = lax.dot_general(qh, kw, NT_DIMS, preferred_element_type=F32)
            lg = lg + bias_ref[h, cls].astype(F32)
            m = jnp.max(lg, axis=-1, keepdims=True)
            p = jnp.exp(lg - m)
            den = jnp.sum(p, axis=-1, keepdims=True)
            outs.append(jnp.dot(p.astype(BF16), vw, preferred_element_type=F32) * (1.0 / den))
        o_ref[pl.ds(q0, rq), :] = jnp.where(lane < HEAD_DIM, outs[0], outs[1]).astype(BF16)
        return c

    lax.fori_loop(0, nrb, blk, 0, unroll=2)


def _na_col_bias(rpb):
    depth = rpb.shape[0]
    col = np.arange(GRID_W)
    cstart = np.clip(col - NA_KW // 2, 0, GRID_W - NA_KW)
    valid_col = (col[None, :] >= cstart[:, None]) & (col[None, :] < cstart[:, None] + NA_KW)
    bj = np.clip(col[None, :] - col[:, None] + NA_KW - 1, 0, 2 * NA_KW - 2)
    sel_col = (bj[:, :, None] == np.arange(2 * NA_KW - 1)) & valid_col[:, :, None]
    t = jnp.einsum('lhij,ckj->lhick', rpb, jnp.asarray(sel_col, F32), precision=lax.Precision.HIGHEST)
    t = jnp.where(jnp.asarray(valid_col)[None, None, None], t, NEG)
    t = jnp.concatenate([t, t], axis=-1)
    return t.reshape(depth, NA_HEADS // 2, 2, 2 * NA_KH - 1, GRID_W, LANES).astype(BF16)


def _na_attention(u3, qw, kw, col_bias, bd, r_blk, w_blk):
    b, s, _ = u3.shape
    rows = s // GRID_W
    rq, wk = r_blk * GRID_W, w_blk * GRID_W
    body = functools.partial(_na_body, rq=rq, wk=wk, rows=rows, r_blk=r_blk, w_blk=w_blk,
                             row_plan=_na_row_plan(r_blk, w_blk, rows))
    return pl.pallas_call(
        body,
        grid=(b, NA_HEADS // 2),
        in_specs=[pl.BlockSpec((None, s, LANES), lambda i, h: (i, 0, U_NAQ // LANES + h)),
                  pl.BlockSpec((None, s, LANES), lambda i, h: (i, 0, U_NAK // LANES + h)),
                  pl.BlockSpec((None, s, LANES), lambda i, h: (i, 0, U_NAV // LANES + h)),
                  pl.BlockSpec((1, LANES), lambda i, h: (0, 0)),
                  pl.BlockSpec((1, LANES), lambda i, h: (0, 0)),
                  pl.BlockSpec((None, 2, 2 * NA_KH - 1, GRID_W, LANES), lambda i, h: (h, 0, 0, 0, 0)),
                  pl.BlockSpec((LANES, LANES), lambda i, h: (0, 0))],
        out_specs=pl.BlockSpec((None, s, LANES), lambda i, h: (i, 0, h)),
        out_shape=SDS((b, s, NA_HEADS * HEAD_DIM), BF16),
        scratch_shapes=[pltpu.VMEM((s, LANES), BF16), pltpu.VMEM((2, 3, rq, wk), BF16)],
        compiler_params=_cparams(("parallel", "parallel")),
        name="na2d",
    )(u3, u3, u3, qw, kw, col_bias, bd)


def _rope(xf, cos, sin):
    lane = lax.broadcasted_iota(I32, xf.shape, 1)
    half = ROPE_DIM // 2
    partner = jnp.where((lane % HEAD_DIM) < half, pltpu.roll(xf, LANES - half, 1), pltpu.roll(xf, half, 1))
    return xf * cos + partner * sin


def _wa_body(sink_ref, q_ref, k_ref, v_ref, cos_ref, sin_ref, qw_ref, kw_ref, bd_ref, o_ref, kd_scr, vd_scr,
             *, qb, wkb):
    s = k_ref.shape[0]
    kvh = pl.program_id(1)
    n = pl.program_id(2)
    bd = bd_ref[...]
    ch = min(512, s)

    @pl.when(n == 0)
    def _():
        lane = lax.broadcasted_iota(I32, (ch, LANES), 1)
        mine = (lane // HEAD_DIM) == kvh

        def prep(i, c):
            s0 = pl.multiple_of(i * ch, ch)
            kf = k_ref[pl.ds(s0, ch), :].astype(F32)
            kf = _rope(_head_rms(kf, kw_ref[...], bd), cos_ref[pl.ds(s0, ch), :], sin_ref[pl.ds(s0, ch), :])
            kd_scr[pl.ds(s0, ch), :] = jnp.where(mine, kf, pltpu.roll(kf, HEAD_DIM, 1)).astype(BF16)
            vf = v_ref[pl.ds(s0, ch), :].astype(F32)
            vd_scr[pl.ds(s0, ch), :] = jnp.where(mine, vf, pltpu.roll(vf, HEAD_DIM, 1)).astype(BF16)
            return c

        lax.fori_loop(0, s // ch, prep, 0)

    lane = lax.broadcasted_iota(I32, (qb, LANES), 1)
    for sub in range(q_ref.shape[0] // qb):
        rs = slice(sub * qb, (sub + 1) * qb)
        q0 = pl.multiple_of(n * q_ref.shape[0] + sub * qb, qb)
        ks = pl.multiple_of(jnp.clip(q0 - WINDOW, 0, s - wkb), LANES)
        cosq = cos_ref[pl.ds(q0, qb), :]
        sinq = sin_ref[pl.ds(q0, qb), :]
        kw = kd_scr[pl.ds(ks, wkb), :]
        vw = vd_scr[pl.ds(ks, wkb), :]
        qpos = q0 + lax.broadcasted_iota(I32, (qb, wkb), 0)
        kpos = ks + lax.broadcasted_iota(I32, (qb, wkb), 1)
        allowed = jnp.abs(kpos - qpos) <= WINDOW
        for half in range(2):
            qf = q_ref[rs, half * LANES:(half + 1) * LANES].astype(F32)
            qn = _rope(_head_rms(qf, qw_ref[...], bd), cosq, sinq) * (HEAD_DIM ** -0.5)
            outs = []
            for gg in range(2):
                qh = jnp.where((lane // HEAD_DIM) == gg, qn, 0.0).astype(BF16)
                lg = lax.dot_general(qh, kw, NT_DIMS, preferred_element_type=F32)
                lg = jnp.where(allowed, lg, NEG)
                sk = sink_ref[0, kvh * 4 + half * 2 + gg]
                m = jnp.maximum(jnp.max(lg, axis=-1, keepdims=True), sk)
                p = jnp.exp(lg - m)
                den = jnp.sum(p, axis=-1, keepdims=True) + jnp.exp(sk - m)
                outs.append(jnp.dot(p.astype(BF16), vw, preferred_element_type=F32) * (1.0 / den))
            o_ref[rs, half * LANES:(half + 1) * LANES] = jnp.where(lane < HEAD_DIM, outs[0], outs[1]).astype(BF16)


def _wa_attention(u3, sink, cos_t, sin_t, qw, kw, bd, qb, qps):
    b, s, _ = u3.shape
    wkb = qb + 2 * WINDOW
    qrows = qb * qps
    body = functools.partial(_wa_body, qb=qb, wkb=wkb)
    return pl.pallas_call(
        body,
        grid=(b, WA_KV_HEADS, s // qrows),
        in_specs=[pl.BlockSpec(memory_space=pltpu.SMEM),
                  pl.BlockSpec((None, qrows, 2 * LANES), lambda i, h, n: (i, n, U_WAQ // (2 * LANES) + h)),
                  pl.BlockSpec((None, s, LANES), lambda i, h, n: (i, 0, U_WAK // LANES)),
                  pl.BlockSpec((None, s, LANES), lambda i, h, n: (i, 0, U_WAV // LANES)),
                  pl.BlockSpec((s, LANES), lambda i, h, n: (0, 0)),
                  pl.BlockSpec((s, LANES), lambda i, h, n: (0, 0)),
                  pl.BlockSpec((1, LANES), lambda i, h, n: (0, 0)),
                  pl.BlockSpec((1, LANES), lambda i, h, n: (0, 0)),
                  pl.BlockSpec((LANES, LANES), lambda i, h, n: (0, 0))],
        out_specs=pl.BlockSpec((None, qrows, 2 * LANES), lambda i, h, n: (i, n, h)),
        out_shape=SDS((b, s, WA_Q_HEADS * HEAD_DIM), BF16),
        scratch_shapes=[pltpu.VMEM((s, LANES), BF16), pltpu.VMEM((s, LANES), BF16)],
        compiler_params=_cparams(("parallel", "arbitrary", "arbitrary")),
        name="swa_gqa",
    )(sink, u3, u3, u3, cos_t, sin_t, qw, kw, bd)


def _rope_tables(s):
    half = ROPE_DIM // 2
    inv_freq = 1.0 / (ROPE_THETA ** (jnp.arange(half, dtype=F32) * 2.0 / ROPE_DIM))
    ang = jnp.arange(s, dtype=F32)[:, None] * inv_freq[None, :]
    cos, sin = jnp.cos(ang), jnp.sin(ang)
    ones = jnp.ones((s, HEAD_DIM - ROPE_DIM), F32)
    cos_h = jnp.concatenate([cos, cos, ones], axis=1)
    sin_h = jnp.concatenate([-sin, sin, 0.0 * ones], axis=1)
    return jnp.tile(cos_h, (1, LANES // HEAD_DIM)), jnp.tile(sin_h, (1, LANES // HEAD_DIM))


def _ssd_body(*refs, reverse, nblk, cpb, final):
    if final:
        (xp_ref, xc_ref, xn_ref, dt_ref, cw_ref, cb_ref, dtb_ref, a_ref, eh_ref,
         yf_ref, z_ref, dv_ref, nw_ref, o_ref, st_scr, cv_scr, g_scr) = refs
    else:
        (xp_ref, xc_ref, xn_ref, dt_ref, cw_ref, cb_ref, dtb_ref, a_ref, eh_ref,
         o_ref, st_scr, cv_scr) = refs
        yf_ref = z_ref = dv_ref = nw_ref = g_scr = None
    bi = pl.program_id(1)
    blk = (nblk - 1 - bi) if reverse else bi
    L = SSM_CHUNK
    rows = cpb * L

    @pl.when(bi == 0)
    def _():
        st_scr[...] = jnp.zeros_like(st_scr)

    has_prev = jnp.where(blk > 0, 1.0, 0.0)
    has_next = jnp.where(blk < nblk - 1, 1.0, 0.0)
    cv_scr[0:8, :] = xp_ref[SLOT_ALIGN - 8:SLOT_ALIGN, :].astype(F32) * has_prev
    cv_scr[8:8 + rows, :] = xc_ref[...].astype(F32)
    cv_scr[8 + rows:16 + rows, :] = xn_ref[0:8, :].astype(F32) * has_next
    for ii in range(cpb):
        ci = (cpb - 1 - ii) if reverse else ii
        _ssd_chunk(ci * L, dt_ref, cw_ref, cb_ref, dtb_ref, a_ref, eh_ref, yf_ref, z_ref, dv_ref, nw_ref,
                   o_ref, st_scr, cv_scr, g_scr, reverse=reverse, final=final)


def _ssd_chunk(r0, dt_ref, cw_ref, cb_ref, dtb_ref, a_ref, eh_ref, yf_ref, z_ref, dv_ref, nw_ref,
               o_ref, st_scr, cv_scr, g_scr, *, reverse, final):
    L = SSM_CHUNK
    sw = SSM_HEADS * HEAD_DIM
    rs = slice(r0, r0 + L)
    left = (SSM_CONV - 1) // 2
    acc = cb_ref[...] + cw_ref[0:1, :] * cv_scr[pl.ds(r0 + 8 - left, L), :]
    for k in range(1, SSM_CONV):
        acc = acc + cw_ref[k:k + 1, :] * cv_scr[pl.ds(r0 + 8 - left + k, L), :]
    xa = acc * _sigmoid(acc)
    xs = xa[:, 0:sw]

    raw = dt_ref[rs, :] + dtb_ref[...]
    dtv = jnp.maximum(raw, 0.0) + jnp.log1p(jnp.exp(-jnp.abs(raw)))
    da = dtv * a_ref[...]
    ri = lax.broadcasted_iota(I32, (L, L), 0)
    cj = lax.broadcasted_iota(I32, (L, L), 1)
    keep = (cj >= ri) if reverse else (cj <= ri)
    tri = jnp.where(keep, 1.0, 0.0).astype(BF16)
    d1, d2, d3 = _split3(da)
    cum = (jnp.dot(tri, d1, preferred_element_type=F32) + jnp.dot(tri, d2, preferred_element_type=F32)
           + jnp.dot(tri, d3, preferred_element_type=F32))
    cum_t = cum.T
    eh = eh_ref[...]
    c1, c2, c3 = _split3(cum)
    cum_x = (jnp.dot(c1, eh, preferred_element_type=F32) + jnp.dot(c2, eh, preferred_element_type=F32)
             + jnp.dot(c3, eh, preferred_element_type=F32))
    t1, t2, t3 = _split3(dtv)
    dt_x = (jnp.dot(t1, eh, preferred_element_type=F32) + jnp.dot(t2, eh, preferred_element_type=F32)
            + jnp.dot(t3, eh, preferred_element_type=F32))
    tot_x = cum_x[0:1, :] if reverse else cum_x[L - 1:L, :]
    xc = xs * dt_x
    xdec = (xc * jnp.exp(tot_x - cum_x)).astype(BF16)
    xcb = xc.astype(BF16)
    ecum = jnp.exp(cum_x)
    etot = jnp.exp(tot_x)
    lane = lax.broadcasted_iota(I32, (L, LANES), 1)
    doff = SSM_HEADS if reverse else 0
    for g in range(2):
        bg = xa[:, sw + g * SSM_STATE: sw + (g + 1) * SSM_STATE]
        cg = xa[:, sw + 2 * SSM_STATE + g * SSM_STATE: sw + 2 * SSM_STATE + (g + 1) * SSM_STATE]
        cgb = cg.astype(BF16)
        cb = lax.dot_general(cgb, bg.astype(BF16), NT_DIMS, preferred_element_type=F32)
        bgt = bg.T.astype(BF16)
        for pp in range(2):
            p = g * 2 + pp
            sl = slice(p * LANES, (p + 1) * LANES)
            ys = []
            for hh in range(2):
                j = doff + 2 * p + hh
                seg = cum[:, j:j + 1] - cum_t[j:j + 1, :]
                dec = jnp.exp(jnp.where(keep, seg, NEG))
                ys.append(jnp.dot((cb * dec).astype(BF16), xcb[:, sl], preferred_element_type=F32))
            y = jnp.where(lane < HEAD_DIM, ys[0], ys[1])
            prev = st_scr[p]
            y = y + jnp.dot(cgb, prev.astype(BF16), preferred_element_type=F32) * ecum[:, sl]
            st_scr[p] = etot[:, sl] * prev + jnp.dot(bgt, xdec[:, sl], preferred_element_type=F32)
            if final:
                y = y + yf_ref[rs, sl] + xs[:, sl] * dv_ref[:, sl]
                zf = z_ref[rs, sl].astype(F32)
                g_scr[:, sl] = y * (zf * _sigmoid(zf))
            else:
                o_ref[rs, sl] = y
    if final:
        gated = g_scr[...]
        ms = jnp.mean(gated * gated, axis=-1, keepdims=True)
        o_ref[rs, :] = (gated * lax.rsqrt(ms + EPS) * nw_ref[...]).astype(BF16)


def _ssd_pass(u3, dt3, cw, cb, dtb, a_vec, eh, extra, reverse, cpb):
    b, s, _ = u3.shape
    L = SSM_CHUNK
    rows = cpb * L
    nblk = s // rows
    hpb = rows // SLOT_ALIGN
    sw = SSM_HEADS * HEAD_DIM
    final = extra is not None

    def cidx(ci):
        return (nblk - 1 - ci) if reverse else ci

    xw = 2 * sw
    in_specs = [pl.BlockSpec((None, SLOT_ALIGN, xw),
                             lambda i, ci: (i, jnp.maximum(cidx(ci) * hpb - 1, 0), U_XBC // xw)),
                pl.BlockSpec((None, rows, xw), lambda i, ci: (i, cidx(ci), U_XBC // xw)),
                pl.BlockSpec((None, SLOT_ALIGN, xw),
                             lambda i, ci: (i, jnp.minimum((cidx(ci) + 1) * hpb, nblk * hpb - 1), U_XBC // xw)),
                pl.BlockSpec((None, None, rows, LANES), lambda i, ci: (0, i, cidx(ci), 0)),
                pl.BlockSpec((8, xw), lambda i, ci: (0, 0)),
                pl.BlockSpec((1, xw), lambda i, ci: (0, 0)),
                pl.BlockSpec((1, LANES), lambda i, ci: (0, 0)),
                pl.BlockSpec((1, LANES), lambda i, ci: (0, 0)),
                pl.BlockSpec((LANES, sw), lambda i, ci: (0, 0))]
    args = [u3, u3, u3, dt3, cw, cb, dtb, a_vec, eh]
    if final:
        y_f, dvec, nw = extra
        in_specs += [pl.BlockSpec((None, rows, sw), lambda i, ci: (i, cidx(ci), 0)),
                     pl.BlockSpec((None, rows, sw), lambda i, ci: (i, cidx(ci), U_Z // sw)),
                     pl.BlockSpec((1, sw), lambda i, ci: (0, 0)),
                     pl.BlockSpec((1, sw), lambda i, ci: (0, 0))]
        args += [y_f, u3, dvec, nw]
    scratch = [pltpu.VMEM((4, SSM_STATE, LANES), F32), pltpu.VMEM((rows + 16, xw), F32)]
    if final:
        scratch.append(pltpu.VMEM((L, sw), F32))
    body = functools.partial(_ssd_body, reverse=reverse, nblk=nblk, cpb=cpb, final=final)
    return pl.pallas_call(
        body,
        grid=(b, nblk),
        in_specs=in_specs,
        out_specs=pl.BlockSpec((None, rows, sw), lambda i, ci: (i, cidx(ci), 0)),
        out_shape=SDS((b, s, sw), BF16 if final else F32),
        scratch_shapes=scratch,
        compiler_params=_cparams(("parallel", "arbitrary")),
        name="ssd_bwd" if reverse else "ssd_fwd",
    )(*args)


def _conf_body(xp_ref, xc_ref, xn_ref, w_ref, b_ref, lnw_ref, lnb_ref, o_ref, scr, ph_scr, *, tc, nt):
    i = pl.program_id(1)
    cwid = CONF_WIDTH
    halo = 16

    def glu(v):
        return v[:, 0:cwid].astype(F32) * _sigmoid(v[:, cwid:2 * cwid].astype(F32))

    scr[0:halo, :] = glu(xp_ref[...]) * jnp.where(i > 0, 1.0, 0.0)
    scr[halo:halo + tc, :] = glu(xc_ref[...])
    scr[halo + tc:2 * halo + tc, :] = glu(xn_ref[...]) * jnp.where(i < nt - 1, 1.0, 0.0)
    first = halo - (CONF_KERNEL - 1) // 2
    acc = b_ref[...]
    for ph in range(SUBLANES):
        part = None
        for row in range(ph, first + CONF_KERNEL, SUBLANES):
            k = row - first
            if 0 <= k < CONF_KERNEL:
                term = w_ref[k:k + 1, :] * scr[pl.ds(row - ph, tc + SUBLANES), :]
                part = term if part is None else part + term
        if ph == 0:
            acc = acc + part[0:tc]
        else:
            ph_scr[...] = part
            acc = acc + ph_scr[pl.ds(ph, tc), :]
    mu = jnp.mean(acc, axis=-1, keepdims=True)
    cen = acc - mu
    var = jnp.mean(cen * cen, axis=-1, keepdims=True)
    y = cen * lax.rsqrt(var + EPS) * lnw_ref[...] + lnb_ref[...]
    o_ref[...] = (y * _sigmoid(y)).astype(BF16)


def _conformer(u3, w, bvec, lnw, lnb, tc):
    b, s, _ = u3.shape
    nt = s // tc
    hpb = tc // SLOT_ALIGN
    cw2 = 2 * CONF_WIDTH
    body = functools.partial(_conf_body, tc=tc, nt=nt)
    return pl.pallas_call(
        body,
        grid=(b, nt),
        in_specs=[pl.BlockSpec((None, SLOT_ALIGN, cw2),
                               lambda i, t: (i, jnp.maximum(t * hpb - 1, 0), U_CONF // cw2)),
                  pl.BlockSpec((None, tc, cw2), lambda i, t: (i, t, U_CONF // cw2)),
                  pl.BlockSpec((None, SLOT_ALIGN, cw2),
                               lambda i, t: (i, jnp.minimum((t + 1) * hpb, nt * hpb - 1), U_CONF // cw2)),
                  pl.BlockSpec((32, CONF_WIDTH), lambda i, t: (0, 0)),
                  pl.BlockSpec((1, CONF_WIDTH), lambda i, t: (0, 0)),
                  pl.BlockSpec((1, CONF_WIDTH), lambda i, t: (0, 0)),
                  pl.BlockSpec((1, CONF_WIDTH), lambda i, t: (0, 0))],
        out_specs=pl.BlockSpec((None, tc, CONF_WIDTH), lambda i, t: (i, t, 0)),
        out_shape=SDS((b, s, CONF_WIDTH), BF16),
        scratch_shapes=[pltpu.VMEM((tc + 32, CONF_WIDTH), F32), pltpu.VMEM((tc + SUBLANES, CONF_WIDTH), F32)],
        compiler_params=_cparams(("parallel", "parallel")),
        name="conformer",
    )(u3, u3, u3, w, bvec, lnw, lnb)


def _outproj_body(oa_ref, ob_ref, oc_ref, od_ref, w_ref, x_ref, fw_ref, wr_ref, xo_ref, h_ref, lg_ref):
    mixed = jnp.concatenate([oa_ref[...], ob_ref[...], oc_ref[...], od_ref[...]], axis=1)
    acc = x_ref[...] + jnp.dot(mixed, w_ref[...], preferred_element_type=F32)
    xo_ref[...] = acc
    ms = jnp.mean(acc * acc, axis=-1, keepdims=True)
    hf = acc * lax.rsqrt(ms + EPS) * fw_ref[...]
    hb = hf.astype(BF16)
    h_ref[...] = hb
    lo = (hf - hb.astype(F32)).astype(BF16)
    wr = wr_ref[...]
    both = jnp.dot(hb, wr, preferred_element_type=F32)
    lg = both[:, 0:LANES] + both[:, LANES:2 * LANES] + jnp.dot(lo, wr[:, 0:LANES], preferred_element_type=F32)
    lg_ref[...] = lg.T[0:N_EXPERTS, :]


def _out_proj(o_a, o_b, o_c, o_d, w_out, layer, x3, fw, wr, tm):
    b, s, _ = x3.shape
    gw = 512
    mix_spec = pl.BlockSpec((None, tm, gw), lambda i, t: (i, t, 0))
    return pl.pallas_call(
        _outproj_body,
        grid=(b, s // tm),
        in_specs=[mix_spec, mix_spec, mix_spec, mix_spec,
                  pl.BlockSpec((None, D_MODEL, D_MODEL), lambda i, t: (layer, 0, 0)),
                  pl.BlockSpec((None, tm, D_MODEL), lambda i, t: (i, t, 0)),
                  pl.BlockSpec((1, D_MODEL), lambda i, t: (0, 0)),
                  pl.BlockSpec((D_MODEL, 2 * LANES), lambda i, t: (0, 0))],
        out_specs=[pl.BlockSpec((None, tm, D_MODEL), lambda i, t: (i, t, 0)),
                   pl.BlockSpec((None, tm, D_MODEL), lambda i, t: (i, t, 0)),
                   pl.BlockSpec((None, N_EXPERTS, tm), lambda i, t: (i, 0, t))],
        out_shape=[SDS((b, s, D_MODEL), F32), SDS((b, s, D_MODEL), BF16), SDS((b, N_EXPERTS, s), F32)],
        compiler_params=_cparams(("parallel", "parallel")),
        name="out_proj",
    )(o_a, o_b, o_c, o_d, w_out, x3, fw, wr)


def _route_body(lg_ref, posm_ref, gate_ref, offs_ref, m_scr, c_scr, *, cap, tk):
    e, s = lg_ref.shape
    cb = 256
    nblk = s // cb
    lg = lg_ref[...]
    mx = jnp.max(lg, axis=0, keepdims=True)
    ex = jnp.exp(lg - mx)
    aff = ex / jnp.sum(ex, axis=0, keepdims=True)
    gate_ref[...] = aff
    bits = pltpu.bitcast(aff, I32)

    def search(i, v):
        cand = v | jnp.left_shift(jnp.int32(1), 30 - i)
        cnt = jnp.sum(jnp.where(bits >= cand, 1.0, 0.0), axis=1, keepdims=True)
        return jnp.where(cnt >= cap, cand, v)

    thr = lax.fori_loop(0, 31, search, jnp.zeros((e, 1), I32))
    gt = bits > thr
    eq = bits == thr
    need = cap - jnp.sum(jnp.where(gt, 1.0, 0.0), axis=1, keepdims=True)
    ri = lax.broadcasted_iota(I32, (cb, cb), 0)
    cj = lax.broadcasted_iota(I32, (cb, cb), 1)
    tri = jnp.where(ri <= cj, 1.0, 0.0).astype(BF16)

    def cumsum_into_c():
        def blk(i, carry):
            o = pl.multiple_of(i * cb, cb)
            inc = jnp.dot(m_scr[:, pl.ds(o, cb)].astype(BF16), tri, preferred_element_type=F32) + carry
            c_scr[:, pl.ds(o, cb)] = inc
            return inc[:, cb - 1:cb]
        lax.fori_loop(0, nblk, blk, jnp.zeros((e, 1), F32))

    eqf = jnp.where(eq, 1.0, 0.0)
    m_scr[...] = eqf
    cumsum_into_c()
    sel = gt | (eq & ((c_scr[...] - eqf) < need))
    self_f = jnp.where(sel, 1.0, 0.0)
    m_scr[...] = self_f
    cumsum_into_c()
    posm_ref[...] = jnp.where(sel, (c_scr[...] - self_f).astype(I32), -1)
    lane = lax.broadcasted_iota(I32, (e, LANES), 1)
    offs = jnp.zeros((e, LANES), I32)
    for c in range(1, s // tk + 1):
        offs = jnp.where(lane == c, c_scr[:, c * tk - 1:c * tk].astype(I32), offs)
    offs_ref[...] = offs


def _route(logits_t, cap, tk):
    b, e, s = logits_t.shape
    body = functools.partial(_route_body, cap=cap, tk=tk)
    return pl.pallas_call(
        body,
        grid=(b,),
        in_specs=[pl.BlockSpec((None, e, s), lambda i: (i, 0, 0))],
        out_specs=[pl.BlockSpec((None, e, s), lambda i: (i, 0, 0)),
                   pl.BlockSpec((None, e, s), lambda i: (i, 0, 0)),
                   pl.BlockSpec((None, e, LANES), lambda i: (i, 0, 0))],
        out_shape=[SDS((b, e, s), I32), SDS((b, e, s), F32), SDS((b, e, LANES), I32)],
        scratch_shapes=[pltpu.VMEM((e, s), F32), pltpu.VMEM((e, s), F32)],
        compiler_params=_cparams(("parallel",)),
        name="route",
    )(logits_t)


def _window_plan(offs_ref, row0, ci, n_exp, row_stride, mp):
    starts, npieces = [], 0
    for k in range(n_exp):
        at = (row0 + k) * row_stride + ci
        off = offs_ref[at]
        n = offs_ref[at + 1] - off
        start = (off // SLOT_ALIGN) * SLOT_ALIGN
        starts.append(start)
        npieces = jnp.maximum(npieces, jnp.where(n > 0, (off - start + n + mp - 1) // mp, 0))
    return starts, npieces


def _gather_body(offs_ref, posm_ref, h_ref, o_ref, *, eg, mp, cap, row_stride):
    bi, gi, ci = pl.program_id(0), pl.program_id(1), pl.program_id(2)
    tk = h_ref.shape[0]

    @pl.when(ci == 0)
    def _():
        o_ref[...] = jnp.zeros_like(o_ref)

    starts, npieces = _window_plan(offs_ref, bi * N_EXPERTS + gi * eg, ci, eg, row_stride, mp)
    riota = lax.broadcasted_iota(I32, (mp, tk), 0)

    def piece(p, carry):
        bands = [jnp.where(riota == posm_ref[k] - (starts[k] + p * mp), 1.0, 0.0).astype(BF16) for k in range(eg)]
        res = jnp.dot(jnp.concatenate(bands, axis=0), h_ref[...], preferred_element_type=F32)
        for k in range(eg):
            dst = pl.multiple_of(jnp.minimum(starts[k] + p * mp, cap), SLOT_ALIGN)
            o_ref[k, pl.ds(dst, mp), :] = o_ref[k, pl.ds(dst, mp), :] + res[k * mp:(k + 1) * mp].astype(BF16)
        return carry

    lax.fori_loop(0, npieces, piece, 0)


def _gather(offs_flat, posm4, h3, cap, tk, eg, mp):
    b, e, _, s = posm4.shape
    nch = s // tk
    body = functools.partial(_gather_body, eg=eg, mp=mp, cap=cap, row_stride=LANES)
    grid_spec = pltpu.PrefetchScalarGridSpec(
        num_scalar_prefetch=1,
        grid=(b, e // eg, nch),
        in_specs=[pl.BlockSpec((None, eg, 1, tk), lambda bi, gi, ci, m: (bi, gi, 0, ci)),
                  pl.BlockSpec((None, tk, D_MODEL), lambda bi, gi, ci, m: (bi, ci, 0))],
        out_specs=pl.BlockSpec((None, eg, cap + mp, D_MODEL), lambda bi, gi, ci, m: (bi, gi, 0, 0)),
    )
    return pl.pallas_call(
        body,
        grid_spec=grid_spec,
        out_shape=SDS((b, e, cap + mp, D_MODEL), BF16),
        compiler_params=_cparams(("parallel", "parallel", "arbitrary")),
        name="moe_gather",
    )(offs_flat, posm4, h3)


def _ffn_body(x_ref, wg_ref, wu_ref, wd_ref, o_ref, acc):
    f = pl.program_id(2)
    x = x_ref[...]
    g = jnp.dot(x, wg_ref[...].astype(BF16), preferred_element_type=F32)
    u = jnp.dot(x, wu_ref[...].astype(BF16), preferred_element_type=F32)
    hid = (g * _sigmoid(g) * u).astype(BF16)

    @pl.when(f == 0)
    def _():
        acc[...] = jnp.zeros_like(acc)

    acc[...] += jnp.dot(hid, wd_ref[...].astype(BF16), preferred_element_type=F32)

    @pl.when(f == pl.num_programs(2) - 1)
    def _():
        o_ref[...] = acc[...].astype(BF16)


def _expert_ffn(xg, w_gate, w_up, w_down, layer, cap, tf):
    b, e = xg.shape[0], xg.shape[1]
    return pl.pallas_call(
        _ffn_body,
        grid=(b, e, EXPERT_FF // tf),
        in_specs=[pl.BlockSpec((None, None, cap, D_MODEL), lambda bi, ei, f: (bi, ei, 0, 0)),
                  pl.BlockSpec((None, None, D_MODEL, tf), lambda bi, ei, f: (layer, ei, 0, f)),
                  pl.BlockSpec((None, None, D_MODEL, tf), lambda bi, ei, f: (layer, ei, 0, f)),
                  pl.BlockSpec((None, None, tf, D_MODEL), lambda bi, ei, f: (layer, ei, f, 0))],
        out_specs=pl.BlockSpec((None, None, cap, D_MODEL), lambda bi, ei, f: (bi, ei, 0, 0)),
        out_shape=SDS((b, e, cap, D_MODEL), BF16),
        scratch_shapes=[pltpu.VMEM((cap, D_MODEL), F32)],
        compiler_params=_cparams(("parallel", "parallel", "arbitrary")),
        name="expert_ffn",
    )(xg, w_gate, w_up, w_down)


def _combine_body(offs_ref, x_ref, posm_ref, gate_ref, y_hbm, o_ref, ybuf, sem, *, mp, cap, row_stride):
    bi, ci = pl.program_id(0), pl.program_id(1)
    nch = pl.num_programs(1)
    step = bi * nch + ci
    slot = step % 2
    tk = x_ref.shape[0]
    riota = lax.broadcasted_iota(I32, (mp, tk), 0)

    def window_copies(b, starts, p, to_slot):
        copies = []
        for k in range(N_EXPERTS):
            src = pl.multiple_of(jnp.minimum(starts[k] + p * mp, cap - mp), SLOT_ALIGN)
            copies.append(pltpu.make_async_copy(y_hbm.at[b, k, pl.ds(src, mp), :],
                                                ybuf.at[to_slot, pl.ds(k * mp, mp), :], sem.at[to_slot]))
        return copies

    starts, npieces = _window_plan(offs_ref, bi * N_EXPERTS, ci, N_EXPERTS, row_stride, mp)

    @pl.when(step == 0)
    def _():
        for cp in window_copies(bi, starts, 0, slot):
            cp.start()

    @pl.when(step + 1 < pl.num_programs(0) * nch)
    def _():
        nb = (step + 1) // nch
        nc = (step + 1) % nch
        nstarts, _ = _window_plan(offs_ref, nb * N_EXPERTS, nc, N_EXPERTS, row_stride, mp)
        for cp in window_copies(nb, nstarts, 0, 1 - slot):
            cp.start()

    def contribution(p):
        bands = []
        for k in range(N_EXPERTS):
            pos = posm_ref[k]
            lo = starts[k] + p * mp
            row = jnp.where((pos >= lo) & (pos < lo + mp), pos - jnp.minimum(lo, cap - mp), -1)
            bands.append(jnp.where(riota == row, gate_ref[k], 0.0).astype(BF16))
        w = jnp.concatenate(bands, axis=0)
        return lax.dot_general(w, ybuf[slot], TN_DIMS, preferred_element_type=F32)

    for cp in window_copies(bi, starts, 0, slot):
        cp.wait()
    o_ref[...] = x_ref[...] + contribution(0)

    def more(p, carry):
        copies = window_copies(bi, starts, p, slot)
        for cp in copies:
            cp.start()
        for cp in copies:
            cp.wait()
        o_ref[...] += contribution(p)
        return carry

    lax.fori_loop(1, npieces, more, 0)


def _combine(offs_flat, x3, posm4, gate4, y4, cap, tk, mp):
    b, s, _ = x3.shape
    body = functools.partial(_combine_body, mp=mp, cap=cap, row_stride=LANES)
    route_spec = pl.BlockSpec((None, N_EXPERTS, 1, tk), lambda bi, ci, m: (bi, 0, 0, ci))
    grid_spec = pltpu.PrefetchScalarGridSpec(
        num_scalar_prefetch=1,
        grid=(b, s // tk),
        in_specs=[pl.BlockSpec((None, tk, D_MODEL), lambda bi, ci, m: (bi, ci, 0)),
                  route_spec, route_spec,
                  pl.BlockSpec(memory_space=pl.ANY)],
        out_specs=pl.BlockSpec((None, tk, D_MODEL), lambda bi, ci, m: (bi, ci, 0)),
        scratch_shapes=[pltpu.VMEM((2, N_EXPERTS * mp, D_MODEL), BF16), pltpu.SemaphoreType.DMA((2,))],
    )
    return pl.pallas_call(
        body,
        grid_spec=grid_spec,
        out_shape=SDS((b, s, D_MODEL), F32),
        compiler_params=_cparams(("arbitrary", "arbitrary")),
        name="moe_combine",
    )(offs_flat, x3, posm4, gate4, y4)


def _repack_w_in(w_in):
    na_w = NA_HEADS * HEAD_DIM
    o = np.cumsum([0, na_w, na_w, na_w, 512, 128, 128, 512, 1024, 16, 1024])
    naq, nak, nav, waq, wak, wav, z, xbc, dt, conf = [w_in[..., o[i]:o[i + 1]] for i in range(10)]
    w_main = jnp.concatenate([xbc, conf, naq, nak, nav, waq, z, wak, wav], axis=-1).astype(BF16)
    w_dt = jnp.pad(dt, ((0, 0), (0, 0), (0, LANES - dt.shape[-1]))).astype(BF16)
    return w_main, w_dt


def _pad_lanes(v, width=LANES):
    return jnp.pad(v, [(0, 0)] * (v.ndim - 1) + [(0, width - v.shape[-1])])


def kernel(x, mix_norm_w, w_in, na_q_norm, na_k_norm, na_rpb, wa_q_norm, wa_k_norm, wa_sink, ssm_conv_w, ssm_conv_b, ssm_dt_bias, ssm_a_log, ssm_d, ssm_norm_w, conf_dw_w, conf_dw_b, conf_ln_w, conf_ln_b, w_out, ffn_norm_w, w_router, w_gate, w_up, w_down):
    b, s, d = x.shape
    depth = w_in.shape[0]
    rows = s // GRID_W
    cap = EC_CAPACITY * s // N_EXPERTS
    r_blk, w_blk = 4, 12
    tk = min(512, s)
    mp = min(96, cap)

    w_main, w_dt = _repack_w_in(w_in)
    w_out_b = w_out.astype(BF16)
    wr_pad = _pad_lanes(w_router)
    wr_hi = wr_pad.astype(BF16)
    wr_cat = jnp.concatenate([wr_hi, (wr_pad - wr_hi.astype(F32)).astype(BF16)], axis=-1)
    cos_t, sin_t = _rope_tables(s)
    bd = jnp.asarray(np.kron(np.eye(LANES // HEAD_DIM), np.full((HEAD_DIM, HEAD_DIM), 1.0 / HEAD_DIM)), BF16)
    eh_np = np.zeros((2, LANES, SSM_HEADS * HEAD_DIM), np.float32)
    for dr in range(2):
        for h in range(SSM_HEADS):
            eh_np[dr, dr * SSM_HEADS + h, h * HEAD_DIM:(h + 1) * HEAD_DIM] = 1.0
    eh = jnp.asarray(eh_np, BF16)
    tile2 = lambda v: jnp.tile(v, (1, LANES // HEAD_DIM))[:, None, :]
    naq_w, nak_w, waq_w, wak_w = tile2(na_q_norm), tile2(na_k_norm), tile2(wa_q_norm), tile2(wa_k_norm)
    dtb = _pad_lanes(ssm_dt_bias.reshape(depth, 1, 2 * SSM_HEADS))
    a_vec = _pad_lanes(-jnp.exp(ssm_a_log.reshape(depth, 1, 2 * SSM_HEADS)))
    dvec = jnp.repeat(ssm_d, HEAD_DIM, axis=-1)[:, None, :]
    conv_w = jnp.pad(ssm_conv_w, ((0, 0), (0, 8 - SSM_CONV), (0, 0)))
    conf_w = jnp.pad(conf_dw_w, ((0, 0), (0, 32 - CONF_KERNEL), (0, 0)))
    col_bias = _na_col_bias(na_rpb)

    for l in range(depth):
        u, dt_raw = _in_proj(x.reshape(b * s, d), mix_norm_w[l][None], w_main, w_dt, l)
        u3 = u.reshape(b, s, U_WIDTH)
        dt3 = dt_raw.reshape(-1, b, s, LANES)
        o_a = _na_attention(u3, naq_w[l], nak_w[l], col_bias[l], bd, r_blk, w_blk)
        o_b = _wa_attention(u3, wa_sink[l][None], cos_t, sin_t, waq_w[l], wak_w[l], bd, 256, 2)
        ssd_args = (u3, dt3, conv_w[l], ssm_conv_b[l][None], dtb[l], a_vec[l])
        y_f = _ssd_pass(*ssd_args, eh[0], None, reverse=False, cpb=4)
        o_c = _ssd_pass(*ssd_args, eh[1], (y_f, dvec[l], ssm_norm_w[l][None]), reverse=True, cpb=4)
        o_d = _conformer(u3, conf_w[l], conf_dw_b[l][None], conf_ln_w[l][None], conf_ln_b[l][None], min(256, s))
        x, h2, logits_t = _out_proj(o_a, o_b, o_c, o_d, w_out_b, l, x, ffn_norm_w[l][None], wr_cat[l], min(512, s))
        posm, gate, offs = _route(logits_t, cap, tk)
        offs_flat = offs.reshape(-1)
        posm4 = posm.reshape(b, N_EXPERTS, 1, s)
        gate4 = gate.reshape(b, N_EXPERTS, 1, s)
        xg = _gather(offs_flat, posm4, h2, cap, tk, 4, mp)
        y4 = _expert_ffn(xg, w_gate, w_up, w_down, l, cap, 512)
        x = _combine(offs_flat, x, posm4, gate4, y4, cap, tk, mp)
    return x
```

```python
import functools

import numpy as np
import jax
import jax.numpy as jnp
from jax import lax
from jax.experimental import pallas as pl
from jax.experimental.pallas import tpu as pltpu

F32 = jnp.float32
BF16 = jnp.bfloat16
I32 = jnp.int32
SDS = jax.ShapeDtypeStruct

D_MODEL = 2048
HEAD_DIM = 64
EPS = 1e-6
NA_HEADS = 8
NA_KH = 8
NA_KW = 16
GRID_W = 64
WA_Q_HEADS = 8
WA_KV_HEADS = 2
WINDOW = 128
ROPE_THETA = 500000.0
ROPE_DIM = HEAD_DIM // 4
SSM_HEADS = 8
SSM_STATE = 128
SSM_CONV = 5
SSM_CHUNK = 128
CONF_WIDTH = 512
CONF_KERNEL = 31
N_EXPERTS = 16
EC_CAPACITY = 2
EXPERT_FF = D_MODEL // 2

U_XBC = 0
U_CONF = 1024
U_NAQ = 2048
U_NAK = 2560
U_NAV = 3072
U_WAQ = 3584
U_Z = 4096
U_WAK = 4608
U_WAV = 4736
U_WIDTH = 4864
LANES = 128
SUBLANES = 8
SLOT_ALIGN = 16
VMEM_LIMIT = 56 * 1024 * 1024

NT_DIMS = (((1,), (1,)), ((), ()))
TN_DIMS = (((0,), (0,)), ((), ()))
NEG = -1e30


def _cparams(sem):
    return pltpu.CompilerParams(dimension_semantics=sem, vmem_limit_bytes=VMEM_LIMIT)


def _split3(v):
    a = v.astype(BF16)
    r = v - a.astype(F32)
    b = r.astype(BF16)
    c = (r - b.astype(F32)).astype(BF16)
    return a, b, c


def _sigmoid(v):
    return 1.0 / (1.0 + jnp.exp(-v))


def _head_rms(xf, w, bd):
    sq = xf * xf
    hi = sq.astype(BF16)
    lo = (sq - hi.astype(F32)).astype(BF16)
    ms = (jnp.dot(hi, bd, preferred_element_type=F32)
          + jnp.dot(lo, bd, preferred_element_type=F32))
    return xf * lax.rsqrt(ms + EPS) * w


def _inproj_body(x_ref, nw_ref, w_ref, wdt_ref, u_ref, dt_ref):
    x = x_ref[...]
    ms = jnp.mean(x * x, axis=-1, keepdims=True)
    h = (x * lax.rsqrt(ms + EPS) * nw_ref[...]).astype(BF16)
    u_ref[...] = jnp.dot(h, w_ref[...], preferred_element_type=F32).astype(BF16)

    @pl.when(pl.program_id(0) == 0)
    def _():
        dt_ref[...] = jnp.dot(h, wdt_ref[...], preferred_element_type=F32)

    @pl.when(pl.program_id(0) != 0)
    def _():
        dt_ref[...] = jnp.zeros_like(dt_ref)


def _in_proj(x2, norm_w, w_main, w_dt, layer, tm=512):
    m = x2.shape[0]
    tn = U_WIDTH // 2
    return pl.pallas_call(
        _inproj_body,
        grid=(U_WIDTH // tn, m // tm),
        in_specs=[pl.BlockSpec((tm, D_MODEL), lambda j, i: (i, 0)),
                  pl.BlockSpec((1, D_MODEL), lambda j, i: (0, 0)),
                  pl.BlockSpec((None, D_MODEL, tn), lambda j, i: (layer, 0, j)),
                  pl.BlockSpec((None, D_MODEL, LANES), lambda j, i: (layer, 0, 0))],
        out_specs=[pl.BlockSpec((tm, tn), lambda j, i: (i, j)),
                   pl.BlockSpec((None, tm, LANES), lambda j, i: (j, i, 0))],
        out_shape=[SDS((m, U_WIDTH), BF16), SDS((U_WIDTH // tn, m, LANES), F32)],
        compiler_params=_cparams(("arbitrary", "arbitrary")),
        name="in_proj",
    )(x2, norm_w, w_main, w_dt)


def _na_row_plan(r_blk, w_blk, rows):
    nrb = rows // r_blk
    plan = []
    for rb in (0, min(1, nrb - 1), nrb - 1):
        w0 = int(np.clip(rb * r_blk - NA_KH // 2, 0, rows - w_blk))
        cls = []
        for rr in range(r_blk):
            r = rb * r_blk + rr
            r0 = int(np.clip(r - NA_KH // 2, 0, rows - NA_KH))
            cls.append(tuple((r0 <= w0 + wi < r0 + NA_KH, w0 + wi - r + NA_KH - 1) for wi in range(w_blk)))
        plan.append(tuple(cls))
    return tuple(plan)


def _na_body(q_ref, k_ref, v_ref, qw_ref, kw_ref, cb_ref, bd_ref, o_ref, kn_scr, bias_ref,
             *, rq, wk, rows, r_blk, w_blk, row_plan):
    s = q_ref.shape[0]
    nrb = rows // r_blk
    bd = bd_ref[...]
    ch = min(512, s)

    lane_t = lax.broadcasted_iota(I32, (GRID_W, LANES), 1)
    masked = jnp.full((GRID_W, LANES), NEG, BF16)
    for h in range(2):
        for cls in range(3):
            for rr in range(r_blk):
                for m2 in range(w_blk // 2):
                    (ok0, i0), (ok1, i1) = row_plan[cls][rr][2 * m2], row_plan[cls][rr][2 * m2 + 1]
                    t0 = cb_ref[h, i0] if ok0 else masked
                    t1 = cb_ref[h, i1] if ok1 else masked
                    bias_ref[h, cls, rr * GRID_W:(rr + 1) * GRID_W, m2 * LANES:(m2 + 1) * LANES] = (
                        jnp.where(lane_t < GRID_W, t0, t1).astype(F32))

    def kprep(i, c):
        s0 = pl.multiple_of(i * ch, ch)
        kf = k_ref[pl.ds(s0, ch), :].astype(F32)
        kn_scr[pl.ds(s0, ch), :] = _head_rms(kf, kw_ref[...], bd).astype(BF16)
        return c

    lax.fori_loop(0, s // ch, kprep, 0)
    lane = lax.broadcasted_iota(I32, (rq, LANES), 1)

    def blk(rb, c):
        q0 = pl.multiple_of(rb * rq, rq)
        w0 = pl.multiple_of(jnp.clip(rb * r_blk - NA_KH // 2, 0, rows - w_blk) * GRID_W, GRID_W)
        cls = jnp.where(rb == 0, 0, jnp.where(rb == nrb - 1, 2, 1))
        qn = _head_rms(q_ref[pl.ds(q0, rq), :].astype(F32), qw_ref[...], bd) * (HEAD_DIM ** -0.5)
        kw = kn_scr[pl.ds(w0, wk), :]
        vw = v_ref[pl.ds(w0, wk), :]
        outs = []
        for h in range(2):
            qh = jnp.where((lane // HEAD_DIM) == h, qn, 0.0).astype(BF16)
            lg = lax.dot_general(qh, kw, NT_DIMS, preferred_element_type=F32)
            lg = lg + bias_ref[h, cls]
            m = jnp.max(lg, axis=-1, keepdims=True)
            p = jnp.exp(lg - m)
            den = jnp.sum(p, axis=-1, keepdims=True)
            outs.append(jnp.dot(p.astype(BF16), vw, preferred_element_type=F32) * (1.0 / den))
        o_ref[pl.ds(q0, rq), :] = jnp.where(lane < HEAD_DIM, outs[0], outs[1]).astype(BF16)
        return c

    lax.fori_loop(0, nrb, blk, 0, unroll=2)


def _na_col_bias(rpb):
    depth = rpb.shape[0]
    col = np.arange(GRID_W)
    cstart = np.clip(col - NA_KW // 2, 0, GRID_W - NA_KW)
    valid_col = (col[None, :] >= cstart[:, None]) & (col[None, :] < cstart[:, None] + NA_KW)
    bj = np.clip(col[None, :] - col[:, None] + NA_KW - 1, 0, 2 * NA_KW - 2)
    sel_col = (bj[:, :, None] == np.arange(2 * NA_KW - 1)) & valid_col[:, :, None]
    t = jnp.einsum('lhij,ckj->lhick', rpb, jnp.asarray(sel_col, F32), precision=lax.Precision.HIGHEST)
    t = jnp.where(jnp.asarray(valid_col)[None, None, None], t, NEG)
    t = jnp.concatenate([t, t], axis=-1)
    return t.reshape(depth, NA_HEADS // 2, 2, 2 * NA_KH - 1, GRID_W, LANES).astype(BF16)


def _na_attention(u3, qw, kw, col_bias, bd, r_blk, w_blk):
    b, s, _ = u3.shape
    rows = s // GRID_W
    rq, wk = r_blk * GRID_W, w_blk * GRID_W
    body = functools.partial(_na_body, rq=rq, wk=wk, rows=rows, r_blk=r_blk, w_blk=w_blk,
                             row_plan=_na_row_plan(r_blk, w_blk, rows))
    return pl.pallas_call(
        body,
        grid=(b, NA_HEADS // 2),
        in_specs=[pl.BlockSpec((None, s, LANES), lambda i, h: (i, 0, U_NAQ // LANES + h)),
                  pl.BlockSpec((None, s, LANES), lambda i, h: (i, 0, U_NAK // LANES + h)),
                  pl.BlockSpec((None, s, LANES), lambda i, h: (i, 0, U_NAV // LANES + h)),
                  pl.BlockSpec((1, LANES), lambda i, h: (0, 0)),
                  pl.BlockSpec((1, LANES), lambda i, h: (0, 0)),
                  pl.BlockSpec((None, 2, 2 * NA_KH - 1, GRID_W, LANES), lambda i, h: (h, 0, 0, 0, 0)),
                  pl.BlockSpec((LANES, LANES), lambda i, h: (0, 0))],
        out_specs=pl.BlockSpec((None, s, LANES), lambda i, h: (i, 0, h)),
        out_shape=SDS((b, s, NA_HEADS * HEAD_DIM), BF16),
        scratch_shapes=[pltpu.VMEM((s, LANES), BF16), pltpu.VMEM((2, 3, rq, wk), F32)],
        compiler_params=_cparams(("parallel", "parallel")),
        name="na2d",
    )(u3, u3, u3, qw, kw, col_bias, bd)


def _rope(xf, cos, sin):
    lane = lax.broadcasted_iota(I32, xf.shape, 1)
    half = ROPE_DIM // 2
    partner = jnp.where((lane % HEAD_DIM) < half, pltpu.roll(xf, LANES - half, 1), pltpu.roll(xf, half, 1))
    return xf * cos + partner * sin


def _wa_body(sink_ref, q_ref, k_ref, v_ref, cos_ref, sin_ref, qw_ref, kw_ref, bd_ref, o_ref, kd_scr, vd_scr,
             *, qb, wkb):
    s = k_ref.shape[0]
    kvh = pl.program_id(1)
    n = pl.program_id(2)
    bd = bd_ref[...]
    ch = min(512, s)

    @pl.when(n == 0)
    def _():
        lane = lax.broadcasted_iota(I32, (ch, LANES), 1)
        mine = (lane // HEAD_DIM) == kvh

        def prep(i, c):
            s0 = pl.multiple_of(i * ch, ch)
            kf = k_ref[pl.ds(s0, ch), :].astype(F32)
            kf = _rope(_head_rms(kf, kw_ref[...], bd), cos_ref[pl.ds(s0, ch), :], sin_ref[pl.ds(s0, ch), :])
            kd_scr[pl.ds(s0, ch), :] = jnp.where(mine, kf, pltpu.roll(kf, HEAD_DIM, 1)).astype(BF16)
            vf = v_ref[pl.ds(s0, ch), :].astype(F32)
            vd_scr[pl.ds(s0, ch), :] = jnp.where(mine, vf, pltpu.roll(vf, HEAD_DIM, 1)).astype(BF16)
            return c

        lax.fori_loop(0, s // ch, prep, 0)

    lane = lax.broadcasted_iota(I32, (qb, LANES), 1)
    for sub in range(q_ref.shape[0] // qb):
        rs = slice(sub * qb, (sub + 1) * qb)
        q0 = pl.multiple_of(n * q_ref.shape[0] + sub * qb, qb)
        ks = pl.multiple_of(jnp.clip(q0 - WINDOW, 0, s - wkb), LANES)
        cosq = cos_ref[pl.ds(q0, qb), :]
        sinq = sin_ref[pl.ds(q0, qb), :]
        kw = kd_scr[pl.ds(ks, wkb), :]
        vw = vd_scr[pl.ds(ks, wkb), :]
        qpos = q0 + lax.broadcasted_iota(I32, (qb, wkb), 0)
        kpos = ks + lax.broadcasted_iota(I32, (qb, wkb), 1)
        allowed = jnp.abs(kpos - qpos) <= WINDOW
        for half in range(2):
            qf = q_ref[rs, half * LANES:(half + 1) * LANES].astype(F32)
            qn = _rope(_head_rms(qf, qw_ref[...], bd), cosq, sinq) * (HEAD_DIM ** -0.5)
            outs = []
            for gg in range(2):
                qh = jnp.where((lane // HEAD_DIM) == gg, qn, 0.0).astype(BF16)
                lg = lax.dot_general(qh, kw, NT_DIMS, preferred_element_type=F32)
                lg = jnp.where(allowed, lg, NEG)
                sk = sink_ref[0, kvh * 4 + half * 2 + gg]
                m = jnp.maximum(jnp.max(lg, axis=-1, keepdims=True), sk)
                p = jnp.exp(lg - m)
                den = jnp.sum(p, axis=-1, keepdims=True) + jnp.exp(sk - m)
                outs.append(jnp.dot(p.astype(BF16), vw, preferred_element_type=F32) * (1.0 / den))
            o_ref[rs, half * LANES:(half + 1) * LANES] = jnp.where(lane < HEAD_DIM, outs[0], outs[1]).astype(BF16)


def _wa_attention(u3, sink, cos_t, sin_t, qw, kw, bd, qb, qps):
    b, s, _ = u3.shape
    wkb = qb + 2 * WINDOW
    qrows = qb * qps
    body = functools.partial(_wa_body, qb=qb, wkb=wkb)
    return pl.pallas_call(
        body,
        grid=(b, WA_KV_HEADS, s // qrows),
        in_specs=[pl.BlockSpec(memory_space=pltpu.SMEM),
                  pl.BlockSpec((None, qrows, 2 * LANES), lambda i, h, n: (i, n, U_WAQ // (2 * LANES) + h)),
                  pl.BlockSpec((None, s, LANES), lambda i, h, n: (i, 0, U_WAK // LANES)),
                  pl.BlockSpec((None, s, LANES), lambda i, h, n: (i, 0, U_WAV // LANES)),
                  pl.BlockSpec((s, LANES), lambda i, h, n: (0, 0)),
                  pl.BlockSpec((s, LANES), lambda i, h, n: (0, 0)),
                  pl.BlockSpec((1, LANES), lambda i, h, n: (0, 0)),
                  pl.BlockSpec((1, LANES), lambda i, h, n: (0, 0)),
                  pl.BlockSpec((LANES, LANES), lambda i, h, n: (0, 0))],
        out_specs=pl.BlockSpec((None, qrows, 2 * LANES), lambda i, h, n: (i, n, h)),
        out_shape=SDS((b, s, WA_Q_HEADS * HEAD_DIM), BF16),
        scratch_shapes=[pltpu.VMEM((s, LANES), BF16), pltpu.VMEM((s, LANES), BF16)],
        compiler_params=_cparams(("parallel", "arbitrary", "arbitrary")),
        name="swa_gqa",
    )(sink, u3, u3, u3, cos_t, sin_t, qw, kw, bd)


def _rope_tables(s):
    half = ROPE_DIM // 2
    inv_freq = 1.0 / (ROPE_THETA ** (jnp.arange(half, dtype=F32) * 2.0 / ROPE_DIM))
    ang = jnp.arange(s, dtype=F32)[:, None] * inv_freq[None, :]
    cos, sin = jnp.cos(ang), jnp.sin(ang)
    ones = jnp.ones((s, HEAD_DIM - ROPE_DIM), F32)
    cos_h = jnp.concatenate([cos, cos, ones], axis=1)
    sin_h = jnp.concatenate([-sin, sin, 0.0 * ones], axis=1)
    return jnp.tile(cos_h, (1, LANES // HEAD_DIM)), jnp.tile(sin_h, (1, LANES // HEAD_DIM))


def _ssd_body(*refs, reverse, nblk, cpb, final):
    if final:
        (xa_ref, dt_ref, dtb_ref, a_ref, eh_ref, yf_ref, z_ref, dv_ref, nw_ref, o_ref, st_scr, g_scr) = refs
    else:
        (xp_ref, xc_ref, xn_ref, dt_ref, cw_ref, cb_ref, dtb_ref, a_ref, eh_ref,
         o_ref, xa_ref, st_scr, cv_scr) = refs
        yf_ref = z_ref = dv_ref = nw_ref = g_scr = None
    bi = pl.program_id(1)
    blk = (nblk - 1 - bi) if reverse else bi
    L = SSM_CHUNK
    rows = cpb * L

    @pl.when(bi == 0)
    def _():
        st_scr[...] = jnp.zeros_like(st_scr)

    if not final:
        has_prev = jnp.where(blk > 0, 1.0, 0.0)
        has_next = jnp.where(blk < nblk - 1, 1.0, 0.0)
        cv_scr[0:8, :] = xp_ref[SLOT_ALIGN - 8:SLOT_ALIGN, :].astype(F32) * has_prev
        cv_scr[8:8 + rows, :] = xc_ref[...].astype(F32)
        cv_scr[8 + rows:16 + rows, :] = xn_ref[0:8, :].astype(F32) * has_next
    for ii in range(cpb):
        ci = (cpb - 1 - ii) if reverse else ii
        r0 = ci * L
        if final:
            xa = xa_ref[r0:r0 + L, :]
        else:
            left = (SSM_CONV - 1) // 2
            acc = cb_ref[...] + cw_ref[0:1, :] * cv_scr[pl.ds(r0 + 8 - left, L), :]
            for k in range(1, SSM_CONV):
                acc = acc + cw_ref[k:k + 1, :] * cv_scr[pl.ds(r0 + 8 - left + k, L), :]
            xa = acc * _sigmoid(acc)
            xa_ref[r0:r0 + L, :] = xa
        _ssd_chunk(r0, xa, dt_ref, dtb_ref, a_ref, eh_ref, yf_ref, z_ref, dv_ref, nw_ref,
                   o_ref, st_scr, g_scr, reverse=reverse, final=final)


def _ssd_chunk(r0, xa, dt_ref, dtb_ref, a_ref, eh_ref, yf_ref, z_ref, dv_ref, nw_ref,
               o_ref, st_scr, g_scr, *, reverse, final):
    L = SSM_CHUNK
    sw = SSM_HEADS * HEAD_DIM
    rs = slice(r0, r0 + L)
    xs = xa[:, 0:sw]

    raw = dt_ref[rs, :] + dtb_ref[...]
    dtv = jnp.maximum(raw, 0.0) + jnp.log1p(jnp.exp(-jnp.abs(raw)))
    da = dtv * a_ref[...]
    ri = lax.broadcasted_iota(I32, (L, L), 0)
    cj = lax.broadcasted_iota(I32, (L, L), 1)
    keep = (cj >= ri) if reverse else (cj <= ri)
    tri = jnp.where(keep, 1.0, 0.0).astype(BF16)
    d1, d2, d3 = _split3(da)
    cum = (jnp.dot(tri, d1, preferred_element_type=F32) + jnp.dot(tri, d2, preferred_element_type=F32)
           + jnp.dot(tri, d3, preferred_element_type=F32))
    cum_t = cum.T
    eh = eh_ref[...]
    c1, c2, c3 = _split3(cum)
    cum_x = (jnp.dot(c1, eh, preferred_element_type=F32) + jnp.dot(c2, eh, preferred_element_type=F32)
             + jnp.dot(c3, eh, preferred_element_type=F32))
    t1, t2, t3 = _split3(dtv)
    dt_x = (jnp.dot(t1, eh, preferred_element_type=F32) + jnp.dot(t2, eh, preferred_element_type=F32)
            + jnp.dot(t3, eh, preferred_element_type=F32))
    tot_x = cum_x[0:1, :] if reverse else cum_x[L - 1:L, :]
    xc = xs * dt_x
    xdec = (xc * jnp.exp(tot_x - cum_x)).astype(BF16)
    xcb = xc.astype(BF16)
    ecum = jnp.exp(cum_x)
    etot = jnp.exp(tot_x)
    lane = lax.broadcasted_iota(I32, (L, LANES), 1)
    doff = SSM_HEADS if reverse else 0
    for g in range(2):
        bg = xa[:, sw + g * SSM_STATE: sw + (g + 1) * SSM_STATE]
        cg = xa[:, sw + 2 * SSM_STATE + g * SSM_STATE: sw + 2 * SSM_STATE + (g + 1) * SSM_STATE]
        cgb = cg.astype(BF16)
        cb = lax.dot_general(cgb, bg.astype(BF16), NT_DIMS, preferred_element_type=F32)
        bgt = bg.T.astype(BF16)
        for pp in range(2):
            p = g * 2 + pp
            sl = slice(p * LANES, (p + 1) * LANES)
            ys = []
            for hh in range(2):
                j = doff + 2 * p + hh
                seg = cum[:, j:j + 1] - cum_t[j:j + 1, :]
                dec = jnp.exp(jnp.where(keep, seg, NEG))
                ys.append(jnp.dot((cb * dec).astype(BF16), xcb[:, sl], preferred_element_type=F32))
            y = jnp.where(lane < HEAD_DIM, ys[0], ys[1])
            prev = st_scr[p]
            y = y + jnp.dot(cgb, prev.astype(BF16), preferred_element_type=F32) * ecum[:, sl]
            st_scr[p] = etot[:, sl] * prev + jnp.dot(bgt, xdec[:, sl], preferred_element_type=F32)
            if final:
                y = y + yf_ref[rs, sl] + xs[:, sl] * dv_ref[:, sl]
                zf = z_ref[rs, sl].astype(F32)
                g_scr[:, sl] = y * (zf * _sigmoid(zf))
            else:
                o_ref[rs, sl] = y
    if final:
        gated = g_scr[...]
        ms = jnp.mean(gated * gated, axis=-1, keepdims=True)
        o_ref[rs, :] = (gated * lax.rsqrt(ms + EPS) * nw_ref[...]).astype(BF16)


def _ssd_pass(u3, dt3, cw, cb, dtb, a_vec, eh, extra, reverse, cpb):
    b, s, _ = u3.shape
    L = SSM_CHUNK
    rows = cpb * L
    nblk = s // rows
    hpb = rows // SLOT_ALIGN
    sw = SSM_HEADS * HEAD_DIM
    final = extra is not None

    def cidx(ci):
        return (nblk - 1 - ci) if reverse else ci

    xw = 2 * sw
    row_spec = lambda width, col: pl.BlockSpec((None, rows, width), lambda i, ci: (i, cidx(ci), col))
    small = lambda r, c: pl.BlockSpec((r, c), lambda i, ci: (0, 0))
    dt_spec = pl.BlockSpec((None, None, rows, LANES), lambda i, ci: (0, i, cidx(ci), 0))
    state = pltpu.VMEM((4, SSM_STATE, LANES), F32)
    if final:
        xa, y_f, dvec, nw = extra
        in_specs = [row_spec(xw, 0), dt_spec, small(1, LANES), small(1, LANES), small(LANES, sw),
                    row_spec(sw, 0), row_spec(sw, U_Z // sw), small(1, sw), small(1, sw)]
        args = [xa, dt3, dtb, a_vec, eh, y_f, u3, dvec, nw]
        out_specs = row_spec(sw, 0)
        out_shape = SDS((b, s, sw), BF16)
        scratch = [state, pltpu.VMEM((L, sw), F32)]
    else:
        in_specs = [pl.BlockSpec((None, SLOT_ALIGN, xw),
                                 lambda i, ci: (i, jnp.maximum(cidx(ci) * hpb - 1, 0), U_XBC // xw)),
                    row_spec(xw, U_XBC // xw),
                    pl.BlockSpec((None, SLOT_ALIGN, xw),
                                 lambda i, ci: (i, jnp.minimum((cidx(ci) + 1) * hpb, nblk * hpb - 1), U_XBC // xw)),
                    dt_spec, small(8, xw), small(1, xw), small(1, LANES), small(1, LANES), small(LANES, sw)]
        args = [u3, u3, u3, dt3, cw, cb, dtb, a_vec, eh]
        out_specs = [row_spec(sw, 0), row_spec(xw, 0)]
        out_shape = [SDS((b, s, sw), F32), SDS((b, s, xw), F32)]
        scratch = [state, pltpu.VMEM((rows + 16, xw), F32)]
    body = functools.partial(_ssd_body, reverse=reverse, nblk=nblk, cpb=cpb, final=final)
    return pl.pallas_call(
        body,
        grid=(b, nblk),
        in_specs=in_specs,
        out_specs=out_specs,
        out_shape=out_shape,
        scratch_shapes=scratch,
        compiler_params=_cparams(("parallel", "arbitrary")),
        name="ssd_bwd" if reverse else "ssd_fwd",
    )(*args)


def _conf_body(xp_ref, xc_ref, xn_ref, w_ref, b_ref, lnw_ref, lnb_ref, o_ref, scr, ph_scr, *, tc, nt):
    i = pl.program_id(1)
    cwid = CONF_WIDTH
    halo = 16

    def glu(v):
        return v[:, 0:cwid].astype(F32) * _sigmoid(v[:, cwid:2 * cwid].astype(F32))

    scr[0:halo, :] = glu(xp_ref[...]) * jnp.where(i > 0, 1.0, 0.0)
    scr[halo:halo + tc, :] = glu(xc_ref[...])
    scr[halo + tc:2 * halo + tc, :] = glu(xn_ref[...]) * jnp.where(i < nt - 1, 1.0, 0.0)
    first = halo - (CONF_KERNEL - 1) // 2
    acc = b_ref[...]
    for ph in range(SUBLANES):
        part = None
        for row in range(ph, first + CONF_KERNEL, SUBLANES):
            k = row - first
            if 0 <= k < CONF_KERNEL:
                term = w_ref[k:k + 1, :] * scr[pl.ds(row - ph, tc + SUBLANES), :]
                part = term if part is None else part + term
        if ph == 0:
            acc = acc + part[0:tc]
        else:
            ph_scr[...] = part
            acc = acc + ph_scr[pl.ds(ph, tc), :]
    mu = jnp.mean(acc, axis=-1, keepdims=True)
    cen = acc - mu
    var = jnp.mean(cen * cen, axis=-1, keepdims=True)
    y = cen * lax.rsqrt(var + EPS) * lnw_ref[...] + lnb_ref[...]
    o_ref[...] = (y * _sigmoid(y)).astype(BF16)


def _conformer(u3, w, bvec, lnw, lnb, tc):
    b, s, _ = u3.shape
    nt = s // tc
    hpb = tc // SLOT_ALIGN
    cw2 = 2 * CONF_WIDTH
    body = functools.partial(_conf_body, tc=tc, nt=nt)
    return pl.pallas_call(
        body,
        grid=(b, nt),
        in_specs=[pl.BlockSpec((None, SLOT_ALIGN, cw2),
                               lambda i, t: (i, jnp.maximum(t * hpb - 1, 0), U_CONF // cw2)),
                  pl.BlockSpec((None, tc, cw2), lambda i, t: (i, t, U_CONF // cw2)),
                  pl.BlockSpec((None, SLOT_ALIGN, cw2),
                               lambda i, t: (i, jnp.minimum((t + 1) * hpb, nt * hpb - 1), U_CONF // cw2)),
                  pl.BlockSpec((32, CONF_WIDTH), lambda i, t: (0, 0)),
                  pl.BlockSpec((1, CONF_WIDTH), lambda i, t: (0, 0)),
                  pl.BlockSpec((1, CONF_WIDTH), lambda i, t: (0, 0)),
                  pl.BlockSpec((1, CONF_WIDTH), lambda i, t: (0, 0))],
        out_specs=pl.BlockSpec((None, tc, CONF_WIDTH), lambda i, t: (i, t, 0)),
        out_shape=SDS((b, s, CONF_WIDTH), BF16),
        scratch_shapes=[pltpu.VMEM((tc + 32, CONF_WIDTH), F32), pltpu.VMEM((tc + SUBLANES, CONF_WIDTH), F32)],
        compiler_params=_cparams(("parallel", "parallel")),
        name="conformer",
    )(u3, u3, u3, w, bvec, lnw, lnb)


def _outproj_body(oa_ref, ob_ref, oc_ref, od_ref, w_ref, x_ref, fw_ref, wr_ref, xo_ref, h_ref, lg_ref):
    mixed = jnp.concatenate([oa_ref[...], ob_ref[...], oc_ref[...], od_ref[...]], axis=1)
    acc = x_ref[...] + jnp.dot(mixed, w_ref[...], preferred_element_type=F32)
    xo_ref[...] = acc
    ms = jnp.mean(acc * acc, axis=-1, keepdims=True)
    hf = acc * lax.rsqrt(ms + EPS) * fw_ref[...]
    hb = hf.astype(BF16)
    h_ref[...] = hb
    lo = (hf - hb.astype(F32)).astype(BF16)
    wr = wr_ref[...]
    both = jnp.dot(hb, wr, preferred_element_type=F32)
    lg = both[:, 0:LANES] + both[:, LANES:2 * LANES] + jnp.dot(lo, wr[:, 0:LANES], preferred_element_type=F32)
    lg_ref[...] = lg.T[0:N_EXPERTS, :]


def _out_proj(o_a, o_b, o_c, o_d, w_out, layer, x3, fw, wr, tm):
    b, s, _ = x3.shape
    gw = 512
    mix_spec = pl.BlockSpec((None, tm, gw), lambda i, t: (i, t, 0))
    return pl.pallas_call(
        _outproj_body,
        grid=(b, s // tm),
        in_specs=[mix_spec, mix_spec, mix_spec, mix_spec,
                  pl.BlockSpec((None, D_MODEL, D_MODEL), lambda i, t: (layer, 0, 0)),
                  pl.BlockSpec((None, tm, D_MODEL), lambda i, t: (i, t, 0)),
                  pl.BlockSpec((1, D_MODEL), lambda i, t: (0, 0)),
                  pl.BlockSpec((D_MODEL, 2 * LANES), lambda i, t: (0, 0))],
        out_specs=[pl.BlockSpec((None, tm, D_MODEL), lambda i, t: (i, t, 0)),
                   pl.BlockSpec((None, tm, D_MODEL), lambda i, t: (i, t, 0)),
                   pl.BlockSpec((None, N_EXPERTS, tm), lambda i, t: (i, 0, t))],
        out_shape=[SDS((b, s, D_MODEL), F32), SDS((b, s, D_MODEL), BF16), SDS((b, N_EXPERTS, s), F32)],
        compiler_params=_cparams(("parallel", "parallel")),
        name="out_proj",
    )(o_a, o_b, o_c, o_d, w_out, x3, fw, wr)


def _route_body(lg_ref, posm_ref, gate_ref, offs_ref, m_scr, c_scr, *, cap, tk):
    e, s = lg_ref.shape
    cb = 256
    nblk = s // cb
    lg = lg_ref[...]
    mx = jnp.max(lg, axis=0, keepdims=True)
    ex = jnp.exp(lg - mx)
    aff = ex / jnp.sum(ex, axis=0, keepdims=True)
    gate_ref[...] = aff
    bits = pltpu.bitcast(aff, I32)

    def search(i, v):
        cand = v | jnp.left_shift(jnp.int32(1), 30 - i)
        cnt = jnp.sum(jnp.where(bits >= cand, 1.0, 0.0), axis=1, keepdims=True)
        return jnp.where(cnt >= cap, cand, v)

    thr = lax.fori_loop(0, 31, search, jnp.zeros((e, 1), I32))
    gt = bits > thr
    eq = bits == thr
    need = cap - jnp.sum(jnp.where(gt, 1.0, 0.0), axis=1, keepdims=True)
    ri = lax.broadcasted_iota(I32, (cb, cb), 0)
    cj = lax.broadcasted_iota(I32, (cb, cb), 1)
    tri = jnp.where(ri <= cj, 1.0, 0.0).astype(BF16)

    def cumsum_into_c():
        def blk(i, carry):
            o = pl.multiple_of(i * cb, cb)
            inc = jnp.dot(m_scr[:, pl.ds(o, cb)].astype(BF16), tri, preferred_element_type=F32) + carry
            c_scr[:, pl.ds(o, cb)] = inc
            return inc[:, cb - 1:cb]
        lax.fori_loop(0, nblk, blk, jnp.zeros((e, 1), F32))

    eqf = jnp.where(eq, 1.0, 0.0)
    m_scr[...] = eqf
    cumsum_into_c()
    sel = gt | (eq & ((c_scr[...] - eqf) < need))
    self_f = jnp.where(sel, 1.0, 0.0)
    m_scr[...] = self_f
    cumsum_into_c()
    posm_ref[...] = jnp.where(sel, (c_scr[...] - self_f).astype(I32), -1)
    lane = lax.broadcasted_iota(I32, (e, LANES), 1)
    offs = jnp.zeros((e, LANES), I32)
    for c in range(1, s // tk + 1):
        offs = jnp.where(lane == c, c_scr[:, c * tk - 1:c * tk].astype(I32), offs)
    offs_ref[...] = offs


def _route(logits_t, cap, tk):
    b, e, s = logits_t.shape
    body = functools.partial(_route_body, cap=cap, tk=tk)
    return pl.pallas_call(
        body,
        grid=(b,),
        in_specs=[pl.BlockSpec((None, e, s), lambda i: (i, 0, 0))],
        out_specs=[pl.BlockSpec((None, e, s), lambda i: (i, 0, 0)),
                   pl.BlockSpec((None, e, s), lambda i: (i, 0, 0)),
                   pl.BlockSpec((None, e, LANES), lambda i: (i, 0, 0))],
        out_shape=[SDS((b, e, s), I32), SDS((b, e, s), F32), SDS((b, e, LANES), I32)],
        scratch_shapes=[pltpu.VMEM((e, s), F32), pltpu.VMEM((e, s), F32)],
        compiler_params=_cparams(("parallel",)),
        name="route",
    )(logits_t)


def _window_plan(offs_ref, row0, ci, n_exp, row_stride, mp):
    starts, npieces = [], 0
    for k in range(n_exp):
        at = (row0 + k) * row_stride + ci
        off = offs_ref[at]
        n = offs_ref[at + 1] - off
        start = (off // SLOT_ALIGN) * SLOT_ALIGN
        starts.append(start)
        npieces = jnp.maximum(npieces, jnp.where(n > 0, (off - start + n + mp - 1) // mp, 0))
    return starts, npieces


def _gather_body(offs_ref, posm_ref, h_ref, o_ref, *, eg, mp, cap, row_stride):
    bi, gi, ci = pl.program_id(0), pl.program_id(1), pl.program_id(2)
    tk = h_ref.shape[0]

    @pl.when(ci == 0)
    def _():
        o_ref[...] = jnp.zeros_like(o_ref)

    starts, npieces = _window_plan(offs_ref, bi * N_EXPERTS + gi * eg, ci, eg, row_stride, mp)
    riota = lax.broadcasted_iota(I32, (mp, tk), 0)

    def piece(p, carry):
        bands = [jnp.where(riota == posm_ref[k] - (starts[k] + p * mp), 1.0, 0.0).astype(BF16) for k in range(eg)]
        res = jnp.dot(jnp.concatenate(bands, axis=0), h_ref[...], preferred_element_type=F32)
        for k in range(eg):
            dst = pl.multiple_of(jnp.minimum(starts[k] + p * mp, cap), SLOT_ALIGN)
            o_ref[k, pl.ds(dst, mp), :] = o_ref[k, pl.ds(dst, mp), :] + res[k * mp:(k + 1) * mp].astype(BF16)
        return carry

    lax.fori_loop(0, npieces, piece, 0)


def _gather(offs_flat, posm4, h3, cap, tk, eg, mp):
    b, e, _, s = posm4.shape
    nch = s // tk
    body = functools.partial(_gather_body, eg=eg, mp=mp, cap=cap, row_stride=LANES)
    grid_spec = pltpu.PrefetchScalarGridSpec(
        num_scalar_prefetch=1,
        grid=(b, e // eg, nch),
        in_specs=[pl.BlockSpec((None, eg, 1, tk), lambda bi, gi, ci, m: (bi, gi, 0, ci)),
                  pl.BlockSpec((None, tk, D_MODEL), lambda bi, gi, ci, m: (bi, ci, 0))],
        out_specs=pl.BlockSpec((None, eg, cap + mp, D_MODEL), lambda bi, gi, ci, m: (bi, gi, 0, 0)),
    )
    return pl.pallas_call(
        body,
        grid_spec=grid_spec,
        out_shape=SDS((b, e, cap + mp, D_MODEL), BF16),
        compiler_params=_cparams(("parallel", "parallel", "arbitrary")),
        name="moe_gather",
    )(offs_flat, posm4, h3)


def _ffn_body(x_ref, wg_ref, wu_ref, wd_ref, o_ref, acc):
    f = pl.program_id(2)
    x = x_ref[...]
    g = jnp.dot(x, wg_ref[...].astype(BF16), preferred_element_type=F32)
    u = jnp.dot(x, wu_ref[...].astype(BF16), preferred_element_type=F32)
    hid = (g * _sigmoid(g) * u).astype(BF16)

    @pl.when(f == 0)
    def _():
        acc[...] = jnp.zeros_like(acc)

    acc[...] += jnp.dot(hid, wd_ref[...].astype(BF16), preferred_element_type=F32)

    @pl.when(f == pl.num_programs(2) - 1)
    def _():
        o_ref[...] = acc[...].astype(BF16)


def _expert_ffn(xg, w_gate, w_up, w_down, layer, cap, tf):
    b, e = xg.shape[0], xg.shape[1]
    return pl.pallas_call(
        _ffn_body,
        grid=(b, e, EXPERT_FF // tf),
        in_specs=[pl.BlockSpec((None, None, cap, D_MODEL), lambda bi, ei, f: (bi, ei, 0, 0)),
                  pl.BlockSpec((None, None, D_MODEL, tf), lambda bi, ei, f: (layer, ei, 0, f)),
                  pl.BlockSpec((None, None, D_MODEL, tf), lambda bi, ei, f: (layer, ei, 0, f)),
                  pl.BlockSpec((None, None, tf, D_MODEL), lambda bi, ei, f: (layer, ei, f, 0))],
        out_specs=pl.BlockSpec((None, None, cap, D_MODEL), lambda bi, ei, f: (bi, ei, 0, 0)),
        out_shape=SDS((b, e, cap, D_MODEL), BF16),
        scratch_shapes=[pltpu.VMEM((cap, D_MODEL), F32)],
        compiler_params=_cparams(("parallel", "parallel", "arbitrary")),
        name="expert_ffn",
    )(xg, w_gate, w_up, w_down)


def _combine_body(offs_ref, x_ref, posm_ref, gate_ref, y_hbm, o_ref, ybuf, sem, *, mp, cap, row_stride):
    bi, ci = pl.program_id(0), pl.program_id(1)
    nch = pl.num_programs(1)
    step = bi * nch + ci
    slot = step % 2
    tk = x_ref.shape[0]
    riota = lax.broadcasted_iota(I32, (mp, tk), 0)

    def window_copies(b, starts, p, to_slot):
        copies = []
        for k in range(N_EXPERTS):
            src = pl.multiple_of(jnp.minimum(starts[k] + p * mp, cap - mp), SLOT_ALIGN)
            copies.append(pltpu.make_async_copy(y_hbm.at[b, k, pl.ds(src, mp), :],
                                                ybuf.at[to_slot, pl.ds(k * mp, mp), :], sem.at[to_slot]))
        return copies

    starts, npieces = _window_plan(offs_ref, bi * N_EXPERTS, ci, N_EXPERTS, row_stride, mp)

    @pl.when(step == 0)
    def _():
        for cp in window_copies(bi, starts, 0, slot):
            cp.start()

    @pl.when(step + 1 < pl.num_programs(0) * nch)
    def _():
        nb = (step + 1) // nch
        nc = (step + 1) % nch
        nstarts, _ = _window_plan(offs_ref, nb * N_EXPERTS, nc, N_EXPERTS, row_stride, mp)
        for cp in window_copies(nb, nstarts, 0, 1 - slot):
            cp.start()

    def contribution(p):
        bands = []
        for k in range(N_EXPERTS):
            pos = posm_ref[k]
            lo = starts[k] + p * mp
            row = jnp.where((pos >= lo) & (pos < lo + mp), pos - jnp.minimum(lo, cap - mp), -1)
            bands.append(jnp.where(riota == row, gate_ref[k], 0.0).astype(BF16))
        w = jnp.concatenate(bands, axis=0)
        return lax.dot_general(w, ybuf[slot], TN_DIMS, preferred_element_type=F32)

    for cp in window_copies(bi, starts, 0, slot):
        cp.wait()
    o_ref[...] = x_ref[...] + contribution(0)

    def more(p, carry):
        copies = window_copies(bi, starts, p, slot)
        for cp in copies:
            cp.start()
        for cp in copies:
            cp.wait()
        o_ref[...] += contribution(p)
        return carry

    lax.fori_loop(1, npieces, more, 0)


def _combine(offs_flat, x3, posm4, gate4, y4, cap, tk, mp):
    b, s, _ = x3.shape
    body = functools.partial(_combine_body, mp=mp, cap=cap, row_stride=LANES)
    route_spec = pl.BlockSpec((None, N_EXPERTS, 1, tk), lambda bi, ci, m: (bi, 0, 0, ci))
    grid_spec = pltpu.PrefetchScalarGridSpec(
        num_scalar_prefetch=1,
        grid=(b, s // tk),
        in_specs=[pl.BlockSpec((None, tk, D_MODEL), lambda bi, ci, m: (bi, ci, 0)),
                  route_spec, route_spec,
                  pl.BlockSpec(memory_space=pl.ANY)],
        out_specs=pl.BlockSpec((None, tk, D_MODEL), lambda bi, ci, m: (bi, ci, 0)),
        scratch_shapes=[pltpu.VMEM((2, N_EXPERTS * mp, D_MODEL), BF16), pltpu.SemaphoreType.DMA((2,))],
    )
    return pl.pallas_call(
        body,
        grid_spec=grid_spec,
        out_shape=SDS((b, s, D_MODEL), F32),
        compiler_params=_cparams(("arbitrary", "arbitrary")),
        name="moe_combine",
    )(offs_flat, x3, posm4, gate4, y4)


def _repack_w_in(w_in):
    na_w = NA_HEADS * HEAD_DIM
    o = np.cumsum([0, na_w, na_w, na_w, 512, 128, 128, 512, 1024, 16, 1024])
    naq, nak, nav, waq, wak, wav, z, xbc, dt, conf = [w_in[..., o[i]:o[i + 1]] for i in range(10)]
    w_main = jnp.concatenate([xbc, conf, naq, nak, nav, waq, z, wak, wav], axis=-1).astype(BF16)
    w_dt = jnp.pad(dt, ((0, 0), (0, 0), (0, LANES - dt.shape[-1]))).astype(BF16)
    return w_main, w_dt


def _pad_lanes(v, width=LANES):
    return jnp.pad(v, [(0, 0)] * (v.ndim - 1) + [(0, width - v.shape[-1])])


def kernel(x, mix_norm_w, w_in, na_q_norm, na_k_norm, na_rpb, wa_q_norm, wa_k_norm, wa_sink, ssm_conv_w, ssm_conv_b, ssm_dt_bias, ssm_a_log, ssm_d, ssm_norm_w, conf_dw_w, conf_dw_b, conf_ln_w, conf_ln_b, w_out, ffn_norm_w, w_router, w_gate, w_up, w_down):
    b, s, d = x.shape
    depth = w_in.shape[0]
    rows = s // GRID_W
    cap = EC_CAPACITY * s // N_EXPERTS
    r_blk, w_blk = 4, 12
    tk = min(512, s)
    mp = min(96, cap)

    w_main, w_dt = _repack_w_in(w_in)
    w_out_b = w_out.astype(BF16)
    wr_pad = _pad_lanes(w_router)
    wr_hi = wr_pad.astype(BF16)
    wr_cat = jnp.concatenate([wr_hi, (wr_pad - wr_hi.astype(F32)).astype(BF16)], axis=-1)
    cos_t, sin_t = _rope_tables(s)
    bd = jnp.asarray(np.kron(np.eye(LANES // HEAD_DIM), np.full((HEAD_DIM, HEAD_DIM), 1.0 / HEAD_DIM)), BF16)
    eh_np = np.zeros((2, LANES, SSM_HEADS * HEAD_DIM), np.float32)
    for dr in range(2):
        for h in range(SSM_HEADS):
            eh_np[dr, dr * SSM_HEADS + h, h * HEAD_DIM:(h + 1) * HEAD_DIM] = 1.0
    eh = jnp.asarray(eh_np, BF16)
    tile2 = lambda v: jnp.tile(v, (1, LANES // HEAD_DIM))[:, None, :]
    naq_w, nak_w, waq_w, wak_w = tile2(na_q_norm), tile2(na_k_norm), tile2(wa_q_norm), tile2(wa_k_norm)
    dtb = _pad_lanes(ssm_dt_bias.reshape(depth, 1, 2 * SSM_HEADS))
    a_vec = _pad_lanes(-jnp.exp(ssm_a_log.reshape(depth, 1, 2 * SSM_HEADS)))
    dvec = jnp.repeat(ssm_d, HEAD_DIM, axis=-1)[:, None, :]
    conv_w = jnp.pad(ssm_conv_w, ((0, 0), (0, 8 - SSM_CONV), (0, 0)))
    conf_w = jnp.pad(conf_dw_w, ((0, 0), (0, 32 - CONF_KERNEL), (0, 0)))
    col_bias = _na_col_bias(na_rpb)

    for l in range(depth):
        u, dt_raw = _in_proj(x.reshape(b * s, d), mix_norm_w[l][None], w_main, w_dt, l)
        u3 = u.reshape(b, s, U_WIDTH)
        dt3 = dt_raw.reshape(-1, b, s, LANES)
        o_a = _na_attention(u3, naq_w[l], nak_w[l], col_bias[l], bd, r_blk, w_blk)
        o_b = _wa_attention(u3, wa_sink[l][None], cos_t, sin_t, waq_w[l], wak_w[l], bd, 256, 2)
        ssd_args = (u3, dt3, conv_w[l], ssm_conv_b[l][None], dtb[l], a_vec[l])
        y_f, xa = _ssd_pass(*ssd_args, eh[0], None, reverse=False, cpb=4)
        o_c = _ssd_pass(*ssd_args, eh[1], (xa, y_f, dvec[l], ssm_norm_w[l][None]), reverse=True, cpb=4)
        o_d = _conformer(u3, conf_w[l], conf_dw_b[l][None], conf_ln_w[l][None], conf_ln_b[l][None], min(256, s))
        x, h2, logits_t = _out_proj(o_a, o_b, o_c, o_d, w_out_b, l, x, ffn_norm_w[l][None], wr_cat[l], min(512, s))
        posm, gate, offs = _route(logits_t, cap, tk)
        offs_flat = offs.reshape(-1)
        posm4 = posm.reshape(b, N_EXPERTS, 1, s)
        gate4 = gate.reshape(b, N_EXPERTS, 1, s)
        xg = _gather(offs_flat, posm4, h2, cap, tk, 4, mp)
        y4 = _expert_ffn(xg, w_gate, w_up, w_down, l, cap, 512)
        x = _combine(offs_flat, x, posm4, gate4, y4, cap, tk, mp)
    return x
```

```python
import functools

import numpy as np
import jax
import jax.numpy as jnp
from jax import lax
from jax.experimental import pallas as pl
from jax.experimental.pallas import tpu as pltpu

F32 = jnp.float32
BF16 = jnp.bfloat16
I32 = jnp.int32
SDS = jax.ShapeDtypeStruct

D_MODEL = 2048
HEAD_DIM = 64
EPS = 1e-6
NA_HEADS = 8
NA_KH = 8
NA_KW = 16
GRID_W = 64
WA_Q_HEADS = 8
WA_KV_HEADS = 2
WINDOW = 128
ROPE_THETA = 500000.0
ROPE_DIM = HEAD_DIM // 4
SSM_HEADS = 8
SSM_STATE = 128
SSM_CONV = 5
SSM_CHUNK = 128
CONF_WIDTH = 512
CONF_KERNEL = 31
N_EXPERTS = 16
EC_CAPACITY = 2
EXPERT_FF = D_MODEL // 2

U_XBC = 0
U_CONF = 1024
U_NAQ = 2048
U_NAK = 2560
U_NAV = 3072
U_WAQ = 3584
U_Z = 4096
U_WAK = 4608
U_WAV = 4736
U_WIDTH = 4864
LANES = 128
SUBLANES = 8
SLOT_ALIGN = 16
VMEM_LIMIT = 56 * 1024 * 1024

NT_DIMS = (((1,), (1,)), ((), ()))
TN_DIMS = (((0,), (0,)), ((), ()))
NEG = -1e30


def _cparams(sem):
    return pltpu.CompilerParams(dimension_semantics=sem, vmem_limit_bytes=VMEM_LIMIT)


def _split3(v):
    a = v.astype(BF16)
    r = v - a.astype(F32)
    b = r.astype(BF16)
    c = (r - b.astype(F32)).astype(BF16)
    return a, b, c


def _sigmoid(v):
    return 1.0 / (1.0 + jnp.exp(-v))


def _head_rms(xf, w, bd):
    sq = xf * xf
    hi = sq.astype(BF16)
    lo = (sq - hi.astype(F32)).astype(BF16)
    ms = (jnp.dot(hi, bd, preferred_element_type=F32)
          + jnp.dot(lo, bd, preferred_element_type=F32))
    return xf * lax.rsqrt(ms + EPS) * w


def _inproj_body(x_ref, nw_ref, w_ref, wdt_ref, u_ref, dt_ref):
    x = x_ref[...]
    ms = jnp.mean(x * x, axis=-1, keepdims=True)
    h = (x * lax.rsqrt(ms + EPS) * nw_ref[...]).astype(BF16)
    u_ref[...] = jnp.dot(h, w_ref[...], preferred_element_type=F32).astype(BF16)

    @pl.when(pl.program_id(0) == 0)
    def _():
        dt_ref[...] = jnp.dot(h, wdt_ref[...], preferred_element_type=F32)

    @pl.when(pl.program_id(0) != 0)
    def _():
        dt_ref[...] = jnp.zeros_like(dt_ref)


def _in_proj(x2, norm_w, w_main, w_dt, layer, tm=512):
    m = x2.shape[0]
    tn = U_WIDTH // 2
    return pl.pallas_call(
        _inproj_body,
        grid=(U_WIDTH // tn, m // tm),
        in_specs=[pl.BlockSpec((tm, D_MODEL), lambda j, i: (i, 0)),
                  pl.BlockSpec((1, D_MODEL), lambda j, i: (0, 0)),
                  pl.BlockSpec((None, D_MODEL, tn), lambda j, i: (layer, 0, j)),
                  pl.BlockSpec((None, D_MODEL, LANES), lambda j, i: (layer, 0, 0))],
        out_specs=[pl.BlockSpec((tm, tn), lambda j, i: (i, j)),
                   pl.BlockSpec((None, tm, LANES), lambda j, i: (j, i, 0))],
        out_shape=[SDS((m, U_WIDTH), BF16), SDS((U_WIDTH // tn, m, LANES), F32)],
        compiler_params=_cparams(("arbitrary", "arbitrary")),
        name="in_proj",
    )(x2, norm_w, w_main, w_dt)


def _na_row_plan(r_blk, w_blk, rows):
    nrb = rows // r_blk
    plan = []
    for rb in (0, min(1, nrb - 1), nrb - 1):
        w0 = int(np.clip(rb * r_blk - NA_KH // 2, 0, rows - w_blk))
        cls = []
        for rr in range(r_blk):
            r = rb * r_blk + rr
            r0 = int(np.clip(r - NA_KH // 2, 0, rows - NA_KH))
            cls.append(tuple((r0 <= w0 + wi < r0 + NA_KH, w0 + wi - r + NA_KH - 1) for wi in range(w_blk)))
        plan.append(tuple(cls))
    return tuple(plan)


def _na_body(q_ref, k_ref, v_ref, qw_ref, kw_ref, cb_ref, bd_ref, o_ref, kn_scr, bias_ref,
             *, rq, wk, rows, r_blk, w_blk, row_plan):
    s = q_ref.shape[0]
    nrb = rows // r_blk
    bd = bd_ref[...]
    ch = min(512, s)

    lane_t = lax.broadcasted_iota(I32, (GRID_W, LANES), 1)
    masked = jnp.full((GRID_W, LANES), NEG, BF16)
    for h in range(2):
        for cls in range(3):
            for rr in range(r_blk):
                for m2 in range(w_blk // 2):
                    (ok0, i0), (ok1, i1) = row_plan[cls][rr][2 * m2], row_plan[cls][rr][2 * m2 + 1]
                    t0 = cb_ref[h, i0] if ok0 else masked
                    t1 = cb_ref[h, i1] if ok1 else masked
                    bias_ref[h, cls, rr * GRID_W:(rr + 1) * GRID_W, m2 * LANES:(m2 + 1) * LANES] = (
                        jnp.where(lane_t < GRID_W, t0, t1).astype(F32))

    def kprep(i, c):
        s0 = pl.multiple_of(i * ch, ch)
        kf = k_ref[pl.ds(s0, ch), :].astype(F32)
        kn_scr[pl.ds(s0, ch), :] = _head_rms(kf, kw_ref[...], bd).astype(BF16)
        return c

    lax.fori_loop(0, s // ch, kprep, 0)
    lane = lax.broadcasted_iota(I32, (rq, LANES), 1)

    def blk(rb, c):
        q0 = pl.multiple_of(rb * rq, rq)
        w0 = pl.multiple_of(jnp.clip(rb * r_blk - NA_KH // 2, 0, rows - w_blk) * GRID_W, GRID_W)
        cls = jnp.where(rb == 0, 0, jnp.where(rb == nrb - 1, 2, 1))
        qn = _head_rms(q_ref[pl.ds(q0, rq), :].astype(F32), qw_ref[...], bd) * (HEAD_DIM ** -0.5)
        kw = kn_scr[pl.ds(w0, wk), :]
        vw = v_ref[pl.ds(w0, wk), :]
        outs = []
        for h in range(2):
            qh = jnp.where((lane // HEAD_DIM) == h, qn, 0.0).astype(BF16)
            lg = lax.dot_general(qh, kw, NT_DIMS, preferred_element_type=F32)
            lg = lg + bias_ref[h, cls]
            m = jnp.max(lg, axis=-1, keepdims=True)
            p = jnp.exp(lg - m)
            den = jnp.sum(p, axis=-1, keepdims=True)
            outs.append(jnp.dot(p.astype(BF16), vw, preferred_element_type=F32) * (1.0 / den))
        o_ref[pl.ds(q0, rq), :] = jnp.where(lane < HEAD_DIM, outs[0], outs[1]).astype(BF16)
        return c

    lax.fori_loop(0, nrb, blk, 0, unroll=4)


def _na_col_bias(rpb):
    depth = rpb.shape[0]
    col = np.arange(GRID_W)
    cstart = np.clip(col - NA_KW // 2, 0, GRID_W - NA_KW)
    valid_col = (col[None, :] >= cstart[:, None]) & (col[None, :] < cstart[:, None] + NA_KW)
    bj = np.clip(col[None, :] - col[:, None] + NA_KW - 1, 0, 2 * NA_KW - 2)
    sel_col = (bj[:, :, None] == np.arange(2 * NA_KW - 1)) & valid_col[:, :, None]
    t = jnp.einsum('lhij,ckj->lhick', rpb, jnp.asarray(sel_col, F32), precision=lax.Precision.HIGHEST)
    t = jnp.where(jnp.asarray(valid_col)[None, None, None], t, NEG)
    t = jnp.concatenate([t, t], axis=-1)
    return t.reshape(depth, NA_HEADS // 2, 2, 2 * NA_KH - 1, GRID_W, LANES).astype(BF16)


def _na_attention(u3, qw, kw, col_bias, bd, r_blk, w_blk):
    b, s, _ = u3.shape
    rows = s // GRID_W
    rq, wk = r_blk * GRID_W, w_blk * GRID_W
    body = functools.partial(_na_body, rq=rq, wk=wk, rows=rows, r_blk=r_blk, w_blk=w_blk,
                             row_plan=_na_row_plan(r_blk, w_blk, rows))
    return pl.pallas_call(
        body,
        grid=(b, NA_HEADS // 2),
        in_specs=[pl.BlockSpec((None, s, LANES), lambda i, h: (i, 0, U_NAQ // LANES + h)),
                  pl.BlockSpec((None, s, LANES), lambda i, h: (i, 0, U_NAK // LANES + h)),
                  pl.BlockSpec((None, s, LANES), lambda i, h: (i, 0, U_NAV // LANES + h)),
                  pl.BlockSpec((1, LANES), lambda i, h: (0, 0)),
                  pl.BlockSpec((1, LANES), lambda i, h: (0, 0)),
                  pl.BlockSpec((None, 2, 2 * NA_KH - 1, GRID_W, LANES), lambda i, h: (h, 0, 0, 0, 0)),
                  pl.BlockSpec((LANES, LANES), lambda i, h: (0, 0))],
        out_specs=pl.BlockSpec((None, s, LANES), lambda i, h: (i, 0, h)),
        out_shape=SDS((b, s, NA_HEADS * HEAD_DIM), BF16),
        scratch_shapes=[pltpu.VMEM((s, LANES), BF16), pltpu.VMEM((2, 3, rq, wk), F32)],
        compiler_params=_cparams(("parallel", "parallel")),
        name="na2d",
    )(u3, u3, u3, qw, kw, col_bias, bd)


def _rope(xf, cos, sin):
    lane = lax.broadcasted_iota(I32, xf.shape, 1)
    half = ROPE_DIM // 2
    partner = jnp.where((lane % HEAD_DIM) < half, pltpu.roll(xf, LANES - half, 1), pltpu.roll(xf, half, 1))
    return xf * cos + partner * sin


def _wa_body(sink_ref, q_ref, k_ref, v_ref, cos_ref, sin_ref, qw_ref, kw_ref, bd_ref, o_ref, kd_scr, vd_scr,
             *, qb, wkb):
    s = k_ref.shape[0]
    kvh = pl.program_id(1)
    n = pl.program_id(2)
    bd = bd_ref[...]
    ch = min(512, s)

    @pl.when((kvh == 0) & (n == 0))
    def _():
        lane = lax.broadcasted_iota(I32, (ch, LANES), 1)

        def prep(i, c):
            s0 = pl.multiple_of(i * ch, ch)
            kf = k_ref[pl.ds(s0, ch), :].astype(F32)
            kf = _rope(_head_rms(kf, kw_ref[...], bd), cos_ref[pl.ds(s0, ch), :], sin_ref[pl.ds(s0, ch), :])
            vf = v_ref[pl.ds(s0, ch), :].astype(F32)
            kr = pltpu.roll(kf, HEAD_DIM, 1)
            vr = pltpu.roll(vf, HEAD_DIM, 1)
            for hv in range(WA_KV_HEADS):
                mine = (lane // HEAD_DIM) == hv
                kd_scr[hv, pl.ds(s0, ch), :] = jnp.where(mine, kf, kr).astype(BF16)
                vd_scr[hv, pl.ds(s0, ch), :] = jnp.where(mine, vf, vr).astype(BF16)
            return c

        lax.fori_loop(0, s // ch, prep, 0)

    lane = lax.broadcasted_iota(I32, (qb, LANES), 1)
    for sub in range(q_ref.shape[0] // qb):
        rs = slice(sub * qb, (sub + 1) * qb)
        q0 = pl.multiple_of(n * q_ref.shape[0] + sub * qb, qb)
        ks = pl.multiple_of(jnp.clip(q0 - WINDOW, 0, s - wkb), LANES)
        cosq = cos_ref[pl.ds(q0, qb), :]
        sinq = sin_ref[pl.ds(q0, qb), :]
        kw = kd_scr[kvh, pl.ds(ks, wkb), :]
        vw = vd_scr[kvh, pl.ds(ks, wkb), :]
        qpos = q0 + lax.broadcasted_iota(I32, (qb, wkb), 0)
        kpos = ks + lax.broadcasted_iota(I32, (qb, wkb), 1)
        allowed = jnp.abs(kpos - qpos) <= WINDOW
        for half in range(2):
            qf = q_ref[rs, half * LANES:(half + 1) * LANES].astype(F32)
            qn = _rope(_head_rms(qf, qw_ref[...], bd), cosq, sinq) * (HEAD_DIM ** -0.5)
            outs = []
            for gg in range(2):
                qh = jnp.where((lane // HEAD_DIM) == gg, qn, 0.0).astype(BF16)
                lg = lax.dot_general(qh, kw, NT_DIMS, preferred_element_type=F32)
                lg = jnp.where(allowed, lg, NEG)
                sk = sink_ref[0, kvh * 4 + half * 2 + gg]
                m = jnp.maximum(jnp.max(lg, axis=-1, keepdims=True), sk)
                p = jnp.exp(lg - m)
                den = jnp.sum(p, axis=-1, keepdims=True) + jnp.exp(sk - m)
                outs.append(jnp.dot(p.astype(BF16), vw, preferred_element_type=F32) * (1.0 / den))
            o_ref[rs, half * LANES:(half + 1) * LANES] = jnp.where(lane < HEAD_DIM, outs[0], outs[1]).astype(BF16)


def _wa_attention(u3, sink, cos_t, sin_t, qw, kw, bd, qb, qps):
    b, s, _ = u3.shape
    wkb = qb + 2 * WINDOW
    qrows = qb * qps
    body = functools.partial(_wa_body, qb=qb, wkb=wkb)
    return pl.pallas_call(
        body,
        grid=(b, WA_KV_HEADS, s // qrows),
        in_specs=[pl.BlockSpec(memory_space=pltpu.SMEM),
                  pl.BlockSpec((None, qrows, 2 * LANES), lambda i, h, n: (i, n, U_WAQ // (2 * LANES) + h)),
                  pl.BlockSpec((None, s, LANES), lambda i, h, n: (i, 0, U_WAK // LANES)),
                  pl.BlockSpec((None, s, LANES), lambda i, h, n: (i, 0, U_WAV // LANES)),
                  pl.BlockSpec((s, LANES), lambda i, h, n: (0, 0)),
                  pl.BlockSpec((s, LANES), lambda i, h, n: (0, 0)),
                  pl.BlockSpec((1, LANES), lambda i, h, n: (0, 0)),
                  pl.BlockSpec((1, LANES), lambda i, h, n: (0, 0)),
                  pl.BlockSpec((LANES, LANES), lambda i, h, n: (0, 0))],
        out_specs=pl.BlockSpec((None, qrows, 2 * LANES), lambda i, h, n: (i, n, h)),
        out_shape=SDS((b, s, WA_Q_HEADS * HEAD_DIM), BF16),
        scratch_shapes=[pltpu.VMEM((WA_KV_HEADS, s, LANES), BF16), pltpu.VMEM((WA_KV_HEADS, s, LANES), BF16)],
        compiler_params=_cparams(("parallel", "arbitrary", "arbitrary")),
        name="swa_gqa",
    )(sink, u3, u3, u3, cos_t, sin_t, qw, kw, bd)


def _rope_tables(s):
    half = ROPE_DIM // 2
    inv_freq = 1.0 / (ROPE_THETA ** (jnp.arange(half, dtype=F32) * 2.0 / ROPE_DIM))
    ang = jnp.arange(s, dtype=F32)[:, None] * inv_freq[None, :]
    cos, sin = jnp.cos(ang), jnp.sin(ang)
    ones = jnp.ones((s, HEAD_DIM - ROPE_DIM), F32)
    cos_h = jnp.concatenate([cos, cos, ones], axis=1)
    sin_h = jnp.concatenate([-sin, sin, 0.0 * ones], axis=1)
    return jnp.tile(cos_h, (1, LANES // HEAD_DIM)), jnp.tile(sin_h, (1, LANES // HEAD_DIM))


def _ssd_body(*refs, reverse, nblk, cpb, final):
    if final:
        (xa_ref, dt_ref, dtb_ref, a_ref, eh_ref, yf_ref, z_ref, dv_ref, nw_ref, o_ref, st_scr, g_scr) = refs
    else:
        (xp_ref, xc_ref, xn_ref, dt_ref, cw_ref, cb_ref, dtb_ref, a_ref, eh_ref,
         o_ref, xa_ref, st_scr, cv_scr) = refs
        yf_ref = z_ref = dv_ref = nw_ref = g_scr = None
    bi = pl.program_id(1)
    blk = (nblk - 1 - bi) if reverse else bi
    L = SSM_CHUNK
    rows = cpb * L

    @pl.when(bi == 0)
    def _():
        st_scr[...] = jnp.zeros_like(st_scr)

    if not final:
        has_prev = jnp.where(blk > 0, 1.0, 0.0)
        has_next = jnp.where(blk < nblk - 1, 1.0, 0.0)
        cv_scr[0:8, :] = xp_ref[SLOT_ALIGN - 8:SLOT_ALIGN, :].astype(F32) * has_prev
        cv_scr[8:8 + rows, :] = xc_ref[...].astype(F32)
        cv_scr[8 + rows:16 + rows, :] = xn_ref[0:8, :].astype(F32) * has_next
    for ii in range(cpb):
        ci = (cpb - 1 - ii) if reverse else ii
        r0 = ci * L
        if final:
            xa = xa_ref[r0:r0 + L, :]
        else:
            left = (SSM_CONV - 1) // 2
            acc = cb_ref[...] + cw_ref[0:1, :] * cv_scr[pl.ds(r0 + 8 - left, L), :]
            for k in range(1, SSM_CONV):
                acc = acc + cw_ref[k:k + 1, :] * cv_scr[pl.ds(r0 + 8 - left + k, L), :]
            xa = acc * _sigmoid(acc)
            xa_ref[r0:r0 + L, :] = xa
        _ssd_chunk(r0, xa, dt_ref, dtb_ref, a_ref, eh_ref, yf_ref, z_ref, dv_ref, nw_ref,
                   o_ref, st_scr, g_scr, reverse=reverse, final=final)


def _ssd_chunk(r0, xa, dt_ref, dtb_ref, a_ref, eh_ref, yf_ref, z_ref, dv_ref, nw_ref,
               o_ref, st_scr, g_scr, *, reverse, final):
    L = SSM_CHUNK
    sw = SSM_HEADS * HEAD_DIM
    rs = slice(r0, r0 + L)
    xs = xa[:, 0:sw]

    raw = dt_ref[rs, :] + dtb_ref[...]
    dtv = jnp.maximum(raw, 0.0) + jnp.log1p(jnp.exp(-jnp.abs(raw)))
    da = dtv * a_ref[...]
    ri = lax.broadcasted_iota(I32, (L, L), 0)
    cj = lax.broadcasted_iota(I32, (L, L), 1)
    keep = (cj >= ri) if reverse else (cj <= ri)
    tri = jnp.where(keep, 1.0, 0.0).astype(BF16)
    d1, d2, d3 = _split3(da)
    cum = (jnp.dot(tri, d1, preferred_element_type=F32) + jnp.dot(tri, d2, preferred_element_type=F32)
           + jnp.dot(tri, d3, preferred_element_type=F32))
    cum_t = cum.T
    eh = eh_ref[...]
    c1, c2, c3 = _split3(cum)
    cum_x = (jnp.dot(c1, eh, preferred_element_type=F32) + jnp.dot(c2, eh, preferred_element_type=F32)
             + jnp.dot(c3, eh, preferred_element_type=F32))
    t1, t2, t3 = _split3(dtv)
    dt_x = (jnp.dot(t1, eh, preferred_element_type=F32) + jnp.dot(t2, eh, preferred_element_type=F32)
            + jnp.dot(t3, eh, preferred_element_type=F32))
    tot_x = cum_x[0:1, :] if reverse else cum_x[L - 1:L, :]
    xc = xs * dt_x
    xdec = (xc * jnp.exp(tot_x - cum_x)).astype(BF16)
    xcb = xc.astype(BF16)
    ecum = jnp.exp(cum_x)
    etot = jnp.exp(tot_x)
    lane = lax.broadcasted_iota(I32, (L, LANES), 1)
    doff = SSM_HEADS if reverse else 0
    for g in range(2):
        bg = xa[:, sw + g * SSM_STATE: sw + (g + 1) * SSM_STATE]
        cg = xa[:, sw + 2 * SSM_STATE + g * SSM_STATE: sw + 2 * SSM_STATE + (g + 1) * SSM_STATE]
        cgb = cg.astype(BF16)
        cb = lax.dot_general(cgb, bg.astype(BF16), NT_DIMS, preferred_element_type=F32)
        bgt = bg.T.astype(BF16)
        for pp in range(2):
            p = g * 2 + pp
            sl = slice(p * LANES, (p + 1) * LANES)
            ys = []
            for hh in range(2):
                j = doff + 2 * p + hh
                seg = cum[:, j:j + 1] - cum_t[j:j + 1, :]
                dec = jnp.exp(jnp.where(keep, seg, NEG))
                ys.append(jnp.dot((cb * dec).astype(BF16), xcb[:, sl], preferred_element_type=F32))
            y = jnp.where(lane < HEAD_DIM, ys[0], ys[1])
            prev = st_scr[p]
            y = y + jnp.dot(cgb, prev.astype(BF16), preferred_element_type=F32) * ecum[:, sl]
            st_scr[p] = etot[:, sl] * prev + jnp.dot(bgt, xdec[:, sl], preferred_element_type=F32)
            if final:
                y = y + yf_ref[rs, sl] + xs[:, sl] * dv_ref[:, sl]
                zf = z_ref[rs, sl].astype(F32)
                g_scr[:, sl] = y * (zf * _sigmoid(zf))
            else:
                o_ref[rs, sl] = y
    if final:
        gated = g_scr[...]
        ms = jnp.mean(gated * gated, axis=-1, keepdims=True)
        o_ref[rs, :] = (gated * lax.rsqrt(ms + EPS) * nw_ref[...]).astype(BF16)


def _ssd_pass(u3, dt3, cw, cb, dtb, a_vec, eh, extra, reverse, cpb):
    b, s, _ = u3.shape
    L = SSM_CHUNK
    rows = cpb * L
    nblk = s // rows
    hpb = rows // SLOT_ALIGN
    sw = SSM_HEADS * HEAD_DIM
    final = extra is not None

    def cidx(ci):
        return (nblk - 1 - ci) if reverse else ci

    xw = 2 * sw
    row_spec = lambda width, col: pl.BlockSpec((None, rows, width), lambda i, ci: (i, cidx(ci), col))
    small = lambda r, c: pl.BlockSpec((r, c), lambda i, ci: (0, 0))
    dt_spec = pl.BlockSpec((None, None, rows, LANES), lambda i, ci: (0, i, cidx(ci), 0))
    state = pltpu.VMEM((4, SSM_STATE, LANES), F32)
    if final:
        xa, y_f, dvec, nw = extra
        in_specs = [row_spec(xw, 0), dt_spec, small(1, LANES), small(1, LANES), small(LANES, sw),
                    row_spec(sw, 0), row_spec(sw, U_Z // sw), small(1, sw), small(1, sw)]
        args = [xa, dt3, dtb, a_vec, eh, y_f, u3, dvec, nw]
        out_specs = row_spec(sw, 0)
        out_shape = SDS((b, s, sw), BF16)
        scratch = [state, pltpu.VMEM((L, sw), F32)]
    else:
        in_specs = [pl.BlockSpec((None, SLOT_ALIGN, xw),
                                 lambda i, ci: (i, jnp.maximum(cidx(ci) * hpb - 1, 0), U_XBC // xw)),
                    row_spec(xw, U_XBC // xw),
                    pl.BlockSpec((None, SLOT_ALIGN, xw),
                                 lambda i, ci: (i, jnp.minimum((cidx(ci) + 1) * hpb, nblk * hpb - 1), U_XBC // xw)),
                    dt_spec, small(8, xw), small(1, xw), small(1, LANES), small(1, LANES), small(LANES, sw)]
        args = [u3, u3, u3, dt3, cw, cb, dtb, a_vec, eh]
        out_specs = [row_spec(sw, 0), row_spec(xw, 0)]
        out_shape = [SDS((b, s, sw), F32), SDS((b, s, xw), F32)]
        scratch = [state, pltpu.VMEM((rows + 16, xw), F32)]
    body = functools.partial(_ssd_body, reverse=reverse, nblk=nblk, cpb=cpb, final=final)
    return pl.pallas_call(
        body,
        grid=(b, nblk),
        in_specs=in_specs,
        out_specs=out_specs,
        out_shape=out_shape,
        scratch_shapes=scratch,
        compiler_params=_cparams(("parallel", "arbitrary")),
        name="ssd_bwd" if reverse else "ssd_fwd",
    )(*args)


def _conf_body(xp_ref, xc_ref, xn_ref, w_ref, b_ref, lnw_ref, lnb_ref, o_ref, scr, ph_scr, *, tc, nt):
    i = pl.program_id(1)
    cwid = CONF_WIDTH
    halo = 16

    def glu(v):
        return v[:, 0:cwid].astype(F32) * _sigmoid(v[:, cwid:2 * cwid].astype(F32))

    scr[0:halo, :] = glu(xp_ref[...]) * jnp.where(i > 0, 1.0, 0.0)
    scr[halo:halo + tc, :] = glu(xc_ref[...])
    scr[halo + tc:2 * halo + tc, :] = glu(xn_ref[...]) * jnp.where(i < nt - 1, 1.0, 0.0)
    first = halo - (CONF_KERNEL - 1) // 2
    acc = b_ref[...]
    for ph in range(SUBLANES):
        part = None
        for row in range(ph, first + CONF_KERNEL, SUBLANES):
            k = row - first
            if 0 <= k < CONF_KERNEL:
                term = w_ref[k:k + 1, :] * scr[pl.ds(row - ph, tc + SUBLANES), :]
                part = term if part is None else part + term
        if ph == 0:
            acc = acc + part[0:tc]
        else:
            ph_scr[...] = part
            acc = acc + ph_scr[pl.ds(ph, tc), :]
    mu = jnp.mean(acc, axis=-1, keepdims=True)
    cen = acc - mu
    var = jnp.mean(cen * cen, axis=-1, keepdims=True)
    y = cen * lax.rsqrt(var + EPS) * lnw_ref[...] + lnb_ref[...]
    o_ref[...] = (y * _sigmoid(y)).astype(BF16)


def _conformer(u3, w, bvec, lnw, lnb, tc):
    b, s, _ = u3.shape
    nt = s // tc
    hpb = tc // SLOT_ALIGN
    cw2 = 2 * CONF_WIDTH
    body = functools.partial(_conf_body, tc=tc, nt=nt)
    return pl.pallas_call(
        body,
        grid=(b, nt),
        in_specs=[pl.BlockSpec((None, SLOT_ALIGN, cw2),
                               lambda i, t: (i, jnp.maximum(t * hpb - 1, 0), U_CONF // cw2)),
                  pl.BlockSpec((None, tc, cw2), lambda i, t: (i, t, U_CONF // cw2)),
                  pl.BlockSpec((None, SLOT_ALIGN, cw2),
                               lambda i, t: (i, jnp.minimum((t + 1) * hpb, nt * hpb - 1), U_CONF // cw2)),
                  pl.BlockSpec((32, CONF_WIDTH), lambda i, t: (0, 0)),
                  pl.BlockSpec((1, CONF_WIDTH), lambda i, t: (0, 0)),
                  pl.BlockSpec((1, CONF_WIDTH), lambda i, t: (0, 0)),
                  pl.BlockSpec((1, CONF_WIDTH), lambda i, t: (0, 0))],
        out_specs=pl.BlockSpec((None, tc, CONF_WIDTH), lambda i, t: (i, t, 0)),
        out_shape=SDS((b, s, CONF_WIDTH), BF16),
        scratch_shapes=[pltpu.VMEM((tc + 32, CONF_WIDTH), F32), pltpu.VMEM((tc + SUBLANES, CONF_WIDTH), F32)],
        compiler_params=_cparams(("parallel", "parallel")),
        name="conformer",
    )(u3, u3, u3, w, bvec, lnw, lnb)


def _outproj_body(oa_ref, ob_ref, oc_ref, od_ref, w_ref, x_ref, fw_ref, wr_ref, xo_ref, h_ref, lg_ref):
    mixed = jnp.concatenate([oa_ref[...], ob_ref[...], oc_ref[...], od_ref[...]], axis=1)
    acc = x_ref[...] + jnp.dot(mixed, w_ref[...], preferred_element_type=F32)
    xo_ref[...] = acc
    ms = jnp.mean(acc * acc, axis=-1, keepdims=True)
    hf = acc * lax.rsqrt(ms + EPS) * fw_ref[...]
    hb = hf.astype(BF16)
    h_ref[...] = hb
    lo = (hf - hb.astype(F32)).astype(BF16)
    wr = wr_ref[...]
    both = jnp.dot(hb, wr, preferred_element_type=F32)
    lg = both[:, 0:LANES] + both[:, LANES:2 * LANES] + jnp.dot(lo, wr[:, 0:LANES], preferred_element_type=F32)
    lg_ref[...] = lg.T[0:N_EXPERTS, :]


def _out_proj(o_a, o_b, o_c, o_d, w_out, layer, x3, fw, wr, tm):
    b, s, _ = x3.shape
    gw = 512
    mix_spec = pl.BlockSpec((None, tm, gw), lambda i, t: (i, t, 0))
    return pl.pallas_call(
        _outproj_body,
        grid=(b, s // tm),
        in_specs=[mix_spec, mix_spec, mix_spec, mix_spec,
                  pl.BlockSpec((None, D_MODEL, D_MODEL), lambda i, t: (layer, 0, 0)),
                  pl.BlockSpec((None, tm, D_MODEL), lambda i, t: (i, t, 0)),
                  pl.BlockSpec((1, D_MODEL), lambda i, t: (0, 0)),
                  pl.BlockSpec((D_MODEL, 2 * LANES), lambda i, t: (0, 0))],
        out_specs=[pl.BlockSpec((None, tm, D_MODEL), lambda i, t: (i, t, 0)),
                   pl.BlockSpec((None, tm, D_MODEL), lambda i, t: (i, t, 0)),
                   pl.BlockSpec((None, N_EXPERTS, tm), lambda i, t: (i, 0, t))],
        out_shape=[SDS((b, s, D_MODEL), F32), SDS((b, s, D_MODEL), BF16), SDS((b, N_EXPERTS, s), F32)],
        compiler_params=_cparams(("parallel", "parallel")),
        name="out_proj",
    )(o_a, o_b, o_c, o_d, w_out, x3, fw, wr)


def _route_body(lg_ref, posm_ref, gate_ref, offs_ref, m_scr, c_scr, *, cap, tk):
    e, s = lg_ref.shape
    cb = 256
    nblk = s // cb
    lg = lg_ref[...]
    mx = jnp.max(lg, axis=0, keepdims=True)
    ex = jnp.exp(lg - mx)
    aff = ex / jnp.sum(ex, axis=0, keepdims=True)
    gate_ref[...] = aff
    bits = pltpu.bitcast(aff, I32)

    def search(i, v):
        cand = v | jnp.left_shift(jnp.int32(1), 30 - i)
        cnt = jnp.sum(jnp.where(bits >= cand, 1.0, 0.0), axis=1, keepdims=True)
        return jnp.where(cnt >= cap, cand, v)

    thr = lax.fori_loop(0, 31, search, jnp.zeros((e, 1), I32))
    gt = bits > thr
    eq = bits == thr
    need = cap - jnp.sum(jnp.where(gt, 1.0, 0.0), axis=1, keepdims=True)
    ri = lax.broadcasted_iota(I32, (cb, cb), 0)
    cj = lax.broadcasted_iota(I32, (cb, cb), 1)
    tri = jnp.where(ri <= cj, 1.0, 0.0).astype(BF16)

    def cumsum_into_c():
        def blk(i, carry):
            o = pl.multiple_of(i * cb, cb)
            inc = jnp.dot(m_scr[:, pl.ds(o, cb)].astype(BF16), tri, preferred_element_type=F32) + carry
            c_scr[:, pl.ds(o, cb)] = inc
            return inc[:, cb - 1:cb]
        lax.fori_loop(0, nblk, blk, jnp.zeros((e, 1), F32))

    eqf = jnp.where(eq, 1.0, 0.0)
    m_scr[...] = eqf
    cumsum_into_c()
    sel = gt | (eq & ((c_scr[...] - eqf) < need))
    self_f = jnp.where(sel, 1.0, 0.0)
    m_scr[...] = self_f
    cumsum_into_c()
    posm_ref[...] = jnp.where(sel, (c_scr[...] - self_f).astype(I32), -1)
    lane = lax.broadcasted_iota(I32, (e, LANES), 1)
    offs = jnp.zeros((e, LANES), I32)
    for c in range(1, s // tk + 1):
        offs = jnp.where(lane == c, c_scr[:, c * tk - 1:c * tk].astype(I32), offs)
    offs_ref[...] = offs


def _route(logits_t, cap, tk):
    b, e, s = logits_t.shape
    body = functools.partial(_route_body, cap=cap, tk=tk)
    return pl.pallas_call(
        body,
        grid=(b,),
        in_specs=[pl.BlockSpec((None, e, s), lambda i: (i, 0, 0))],
        out_specs=[pl.BlockSpec((None, e, s), lambda i: (i, 0, 0)),
                   pl.BlockSpec((None, e, s), lambda i: (i, 0, 0)),
                   pl.BlockSpec((None, e, LANES), lambda i: (i, 0, 0))],
        out_shape=[SDS((b, e, s), I32), SDS((b, e, s), F32), SDS((b, e, LANES), I32)],
        scratch_shapes=[pltpu.VMEM((e, s), F32), pltpu.VMEM((e, s), F32)],
        compiler_params=_cparams(("parallel",)),
        name="route",
    )(logits_t)


def _window_plan(offs_ref, row0, ci, n_exp, row_stride, mp):
    starts, npieces = [], 0
    for k in range(n_exp):
        at = (row0 + k) * row_stride + ci
        off = offs_ref[at]
        n = offs_ref[at + 1] - off
        start = (off // SLOT_ALIGN) * SLOT_ALIGN
        starts.append(start)
        npieces = jnp.maximum(npieces, jnp.where(n > 0, (off - start + n + mp - 1) // mp, 0))
    return starts, npieces


def _gather_body(offs_ref, posm_ref, h_ref, o_ref, *, eg, mp, cap, row_stride):
    bi, gi, ci = pl.program_id(0), pl.program_id(1), pl.program_id(2)
    tk = h_ref.shape[0]

    @pl.when(ci == 0)
    def _():
        o_ref[...] = jnp.zeros_like(o_ref)

    starts, npieces = _window_plan(offs_ref, bi * N_EXPERTS + gi * eg, ci, eg, row_stride, mp)
    riota = lax.broadcasted_iota(I32, (mp, tk), 0)

    def piece(p, carry):
        bands = [jnp.where(riota == posm_ref[k] - (starts[k] + p * mp), 1.0, 0.0).astype(BF16) for k in range(eg)]
        res = jnp.dot(jnp.concatenate(bands, axis=0), h_ref[...], preferred_element_type=F32)
        for k in range(eg):
            dst = pl.multiple_of(jnp.minimum(starts[k] + p * mp, cap), SLOT_ALIGN)
            o_ref[k, pl.ds(dst, mp), :] = o_ref[k, pl.ds(dst, mp), :] + res[k * mp:(k + 1) * mp].astype(BF16)
        return carry

    lax.fori_loop(0, npieces, piece, 0)


def _gather(offs_flat, posm4, h3, cap, tk, eg, mp):
    b, e, _, s = posm4.shape
    nch = s // tk
    body = functools.partial(_gather_body, eg=eg, mp=mp, cap=cap, row_stride=LANES)
    grid_spec = pltpu.PrefetchScalarGridSpec(
        num_scalar_prefetch=1,
        grid=(b, e // eg, nch),
        in_specs=[pl.BlockSpec((None, eg, 1, tk), lambda bi, gi, ci, m: (bi, gi, 0, ci)),
                  pl.BlockSpec((None, tk, D_MODEL), lambda bi, gi, ci, m: (bi, ci, 0))],
        out_specs=pl.BlockSpec((None, eg, cap + mp, D_MODEL), lambda bi, gi, ci, m: (bi, gi, 0, 0)),
    )
    return pl.pallas_call(
        body,
        grid_spec=grid_spec,
        out_shape=SDS((b, e, cap + mp, D_MODEL), BF16),
        compiler_params=_cparams(("parallel", "parallel", "arbitrary")),
        name="moe_gather",
    )(offs_flat, posm4, h3)


def _ffn_body(x_ref, wg_ref, wu_ref, wd_ref, o_ref, acc):
    f = pl.program_id(2)
    x = x_ref[...]
    g = jnp.dot(x, wg_ref[...].astype(BF16), preferred_element_type=F32)
    u = jnp.dot(x, wu_ref[...].astype(BF16), preferred_element_type=F32)
    hid = (g * _sigmoid(g) * u).astype(BF16)

    @pl.when(f == 0)
    def _():
        acc[...] = jnp.zeros_like(acc)

    acc[...] += jnp.dot(hid, wd_ref[...].astype(BF16), preferred_element_type=F32)

    @pl.when(f == pl.num_programs(2) - 1)
    def _():
        o_ref[...] = acc[...].astype(BF16)


def _expert_ffn(xg, w_gate, w_up, w_down, layer, cap, tf):
    b, e = xg.shape[0], xg.shape[1]
    return pl.pallas_call(
        _ffn_body,
        grid=(b, e, EXPERT_FF // tf),
        in_specs=[pl.BlockSpec((None, None, cap, D_MODEL), lambda bi, ei, f: (bi, ei, 0, 0)),
                  pl.BlockSpec((None, None, D_MODEL, tf), lambda bi, ei, f: (layer, ei, 0, f)),
                  pl.BlockSpec((None, None, D_MODEL, tf), lambda bi, ei, f: (layer, ei, 0, f)),
                  pl.BlockSpec((None, None, tf, D_MODEL), lambda bi, ei, f: (layer, ei, f, 0))],
        out_specs=pl.BlockSpec((None, None, cap, D_MODEL), lambda bi, ei, f: (bi, ei, 0, 0)),
        out_shape=SDS((b, e, cap, D_MODEL), BF16),
        scratch_shapes=[pltpu.VMEM((cap, D_MODEL), F32)],
        compiler_params=_cparams(("parallel", "parallel", "arbitrary")),
        name="expert_ffn",
    )(xg, w_gate, w_up, w_down)


def _combine_body(offs_ref, x_ref, posm_ref, gate_ref, y_hbm, o_ref, ybuf, sem, *, mp, cap, row_stride):
    bi, ci = pl.program_id(0), pl.program_id(1)
    nch = pl.num_programs(1)
    step = bi * nch + ci
    slot = step % 2
    tk = x_ref.shape[0]
    riota = lax.broadcasted_iota(I32, (mp, tk), 0)

    def window_copies(b, starts, p, to_slot):
        copies = []
        for k in range(N_EXPERTS):
            src = pl.multiple_of(jnp.minimum(starts[k] + p * mp, cap - mp), SLOT_ALIGN)
            copies.append(pltpu.make_async_copy(y_hbm.at[b, k, pl.ds(src, mp), :],
                                                ybuf.at[to_slot, pl.ds(k * mp, mp), :], sem.at[to_slot]))
        return copies

    starts, npieces = _window_plan(offs_ref, bi * N_EXPERTS, ci, N_EXPERTS, row_stride, mp)

    @pl.when(step == 0)
    def _():
        for cp in window_copies(bi, starts, 0, slot):
            cp.start()

    @pl.when(step + 1 < pl.num_programs(0) * nch)
    def _():
        nb = (step + 1) // nch
        nc = (step + 1) % nch
        nstarts, _ = _window_plan(offs_ref, nb * N_EXPERTS, nc, N_EXPERTS, row_stride, mp)
        for cp in window_copies(nb, nstarts, 0, 1 - slot):
            cp.start()

    def contribution(p):
        bands = []
        for k in range(N_EXPERTS):
            pos = posm_ref[k]
            lo = starts[k] + p * mp
            row = jnp.where((pos >= lo) & (pos < lo + mp), pos - jnp.minimum(lo, cap - mp), -1)
            bands.append(jnp.where(riota == row, gate_ref[k], 0.0).astype(BF16))
        w = jnp.concatenate(bands, axis=0)
        return lax.dot_general(w, ybuf[slot], TN_DIMS, preferred_element_type=F32)

    for cp in window_copies(bi, starts, 0, slot):
        cp.wait()
    o_ref[...] = x_ref[...] + contribution(0)

    def more(p, carry):
        copies = window_copies(bi, starts, p, slot)
        for cp in copies:
            cp.start()
        for cp in copies:
            cp.wait()
        o_ref[...] += contribution(p)
        return carry

    lax.fori_loop(1, npieces, more, 0)


def _combine(offs_flat, x3, posm4, gate4, y4, cap, tk, mp):
    b, s, _ = x3.shape
    body = functools.partial(_combine_body, mp=mp, cap=cap, row_stride=LANES)
    route_spec = pl.BlockSpec((None, N_EXPERTS, 1, tk), lambda bi, ci, m: (bi, 0, 0, ci))
    grid_spec = pltpu.PrefetchScalarGridSpec(
        num_scalar_prefetch=1,
        grid=(b, s // tk),
        in_specs=[pl.BlockSpec((None, tk, D_MODEL), lambda bi, ci, m: (bi, ci, 0)),
                  route_spec, route_spec,
                  pl.BlockSpec(memory_space=pl.ANY)],
        out_specs=pl.BlockSpec((None, tk, D_MODEL), lambda bi, ci, m: (bi, ci, 0)),
        scratch_shapes=[pltpu.VMEM((2, N_EXPERTS * mp, D_MODEL), BF16), pltpu.SemaphoreType.DMA((2,))],
    )
    return pl.pallas_call(
        body,
        grid_spec=grid_spec,
        out_shape=SDS((b, s, D_MODEL), F32),
        compiler_params=_cparams(("arbitrary", "arbitrary")),
        name="moe_combine",
    )(offs_flat, x3, posm4, gate4, y4)


def _repack_w_in(w_in):
    na_w = NA_HEADS * HEAD_DIM
    o = np.cumsum([0, na_w, na_w, na_w, 512, 128, 128, 512, 1024, 16, 1024])
    naq, nak, nav, waq, wak, wav, z, xbc, dt, conf = [w_in[..., o[i]:o[i + 1]] for i in range(10)]
    w_main = jnp.concatenate([xbc, conf, naq, nak, nav, waq, z, wak, wav], axis=-1).astype(BF16)
    w_dt = jnp.pad(dt, ((0, 0), (0, 0), (0, LANES - dt.shape[-1]))).astype(BF16)
    return w_main, w_dt


def _pad_lanes(v, width=LANES):
    return jnp.pad(v, [(0, 0)] * (v.ndim - 1) + [(0, width - v.shape[-1])])


def kernel(x, mix_norm_w, w_in, na_q_norm, na_k_norm, na_rpb, wa_q_norm, wa_k_norm, wa_sink, ssm_conv_w, ssm_conv_b, ssm_dt_bias, ssm_a_log, ssm_d, ssm_norm_w, conf_dw_w, conf_dw_b, conf_ln_w, conf_ln_b, w_out, ffn_norm_w, w_router, w_gate, w_up, w_down):
    b, s, d = x.shape
    depth = w_in.shape[0]
    rows = s // GRID_W
    cap = EC_CAPACITY * s // N_EXPERTS
    r_blk, w_blk = 4, 12
    tk = min(512, s)
    mp = min(96, cap)

    w_main, w_dt = _repack_w_in(w_in)
    w_out_b = w_out.astype(BF16)
    wr_pad = _pad_lanes(w_router)
    wr_hi = wr_pad.astype(BF16)
    wr_cat = jnp.concatenate([wr_hi, (wr_pad - wr_hi.astype(F32)).astype(BF16)], axis=-1)
    cos_t, sin_t = _rope_tables(s)
    bd = jnp.asarray(np.kron(np.eye(LANES // HEAD_DIM), np.full((HEAD_DIM, HEAD_DIM), 1.0 / HEAD_DIM)), BF16)
    eh_np = np.zeros((2, LANES, SSM_HEADS * HEAD_DIM), np.float32)
    for dr in range(2):
        for h in range(SSM_HEADS):
            eh_np[dr, dr * SSM_HEADS + h, h * HEAD_DIM:(h + 1) * HEAD_DIM] = 1.0
    eh = jnp.asarray(eh_np, BF16)
    tile2 = lambda v: jnp.tile(v, (1, LANES // HEAD_DIM))[:, None, :]
    naq_w, nak_w, waq_w, wak_w = tile2(na_q_norm), tile2(na_k_norm), tile2(wa_q_norm), tile2(wa_k_norm)
    dtb = _pad_lanes(ssm_dt_bias.reshape(depth, 1, 2 * SSM_HEADS))
    a_vec = _pad_lanes(-jnp.exp(ssm_a_log.reshape(depth, 1, 2 * SSM_HEADS)))
    dvec = jnp.repeat(ssm_d, HEAD_DIM, axis=-1)[:, None, :]
    conv_w = jnp.pad(ssm_conv_w, ((0, 0), (0, 8 - SSM_CONV), (0, 0)))
    conf_w = jnp.pad(conf_dw_w, ((0, 0), (0, 32 - CONF_KERNEL), (0, 0)))
    col_bias = _na_col_bias(na_rpb)

    for l in range(depth):
        u, dt_raw = _in_proj(x.reshape(b * s, d), mix_norm_w[l][None], w_main, w_dt, l)
        u3 = u.reshape(b, s, U_WIDTH)
        dt3 = dt_raw.reshape(-1, b, s, LANES)
        o_a = _na_attention(u3, naq_w[l], nak_w[l], col_bias[l], bd, r_blk, w_blk)
        o_b = _wa_attention(u3, wa_sink[l][None], cos_t, sin_t, waq_w[l], wak_w[l], bd, 256, 2)
        ssd_args = (u3, dt3, conv_w[l], ssm_conv_b[l][None], dtb[l], a_vec[l])
        y_f, xa = _ssd_pass(*ssd_args, eh[0], None, reverse=False, cpb=4)
        o_c = _ssd_pass(*ssd_args, eh[1], (xa, y_f, dvec[l], ssm_norm_w[l][None]), reverse=True, cpb=4)
        o_d = _conformer(u3, conf_w[l], conf_dw_b[l][None], conf_ln_w[l][None], conf_ln_b[l][None], min(512, s))
        x, h2, logits_t = _out_proj(o_a, o_b, o_c, o_d, w_out_b, l, x, ffn_norm_w[l][None], wr_cat[l], min(512, s))
        posm, gate, offs = _route(logits_t, cap, tk)
        offs_flat = offs.reshape(-1)
        posm4 = posm.reshape(b, N_EXPERTS, 1, s)
        gate4 = gate.reshape(b, N_EXPERTS, 1, s)
        xg = _gather(offs_flat, posm4, h2, cap, tk, 4, mp)
        y4 = _expert_ffn(xg, w_gate, w_up, w_down, l, cap, 512)
        x = _combine(offs_flat, x, posm4, gate4, y4, cap, tk, mp)
    return x
```
